```python
import jax, jax.numpy as jnp
from jax import lax
import numpy as np

D_MODEL = 2048
BATCH = 2
SEQ = 16384
DEPTH = 4

CONV_CH = 512
CONV_K = 31
SC_CH = 512
SC_K = 3
N_HEADS = 12
HEAD_DIM = 64
ATT_W = N_HEADS * HEAD_DIM
DIL_GROUPS = ((128, 1), (512, 4), (2048, 16))
HEADS_PER_GROUP = N_HEADS // len(DIL_GROUPS)
ATT_OUT = HEADS_PER_GROUP * HEAD_DIM
Q_BLOCK = 64
N_BRANCH = 3
D_FF = -(-8 * D_MODEL // (3 * 256)) * 256
N_IN = 2 * CONV_CH + 3 * SC_CH + 3 * ATT_W + N_BRANCH * D_MODEL
NORM_EPS = 1e-6
LN_EPS = 1e-5
NEG = -1e30

kernel_name = "hybrid_gated_conformer_shortconv_dilated_alibi_encoder"


def rmsnorm(x, g):
    xf = x.astype(jnp.float32)
    y = xf * lax.rsqrt(jnp.mean(xf * xf, axis=-1, keepdims=True) + NORM_EPS)
    return (y * g.astype(jnp.float32)).astype(x.dtype)


def layernorm(x, g, b):
    xf = x.astype(jnp.float32)
    mu = jnp.mean(xf, axis=-1, keepdims=True)
    var = jnp.mean(jnp.square(xf - mu), axis=-1, keepdims=True)
    y = (xf - mu) * lax.rsqrt(var + LN_EPS)
    return (y * g.astype(jnp.float32) + b.astype(jnp.float32)).astype(x.dtype)


def depthwise_conv(x, w, pad):
    c = x.shape[-1]
    return lax.conv_general_dilated(
        x, w[:, None, :].astype(x.dtype), window_strides=(1,), padding=[(pad, pad)],
        dimension_numbers=("NWC", "WIO", "NWC"), feature_group_count=c)


def alibi_slopes(n):
    return (2.0 ** (-8.0 * np.arange(1, n + 1) / n)).astype(np.float32)


def dilated_window_attention(q, k, v, slopes, dilation, half):
    bt, s, h, dh = q.shape
    L = s // dilation
    nb = -(-L // Q_BLOCK)
    lp = nb * Q_BLOCK
    kb_len = Q_BLOCK + 2 * half

    def split(t):
        return t.reshape(bt, L, dilation, h, dh).transpose(0, 3, 2, 1, 4)

    qs, ks, vs = split(q), split(k), split(v)
    qs = jnp.pad(qs, ((0, 0), (0, 0), (0, 0), (0, lp - L), (0, 0)))
    pad_k = ((0, 0), (0, 0), (0, 0), (half, lp - L + half), (0, 0))
    ks = jnp.pad(ks, pad_k)
    vs = jnp.pad(vs, pad_k)
    idx = np.arange(nb)[:, None] * Q_BLOCK + np.arange(kb_len)[None, :]
    kb = jnp.take(ks, idx, axis=3)
    vb = jnp.take(vs, idx, axis=3)
    qb = qs.reshape(bt, h, dilation, nb, Q_BLOCK, dh)
    scores = jnp.einsum("bhrnqd,bhrnkd->bhrnqk", qb.astype(jnp.float32),
                        kb.astype(jnp.float32)) * (HEAD_DIM ** -0.5)
    rel = np.arange(kb_len)[None, :] - half - np.arange(Q_BLOCK)[:, None]
    kpos = idx - half
    valid = (np.abs(rel) <= half)[None] & ((kpos >= 0) & (kpos < L))[:, None, :]
    dist = (dilation * np.abs(rel)).astype(np.float32)
    bias = -slopes[:, None, None] * dist[None]
    scores = jnp.where(valid, scores + bias[:, None, None], NEG)
    lse = jax.nn.logsumexp(scores, axis=-1)
    p = jnp.exp(scores - lse[..., None])
    out = jnp.einsum("bhrnqk,bhrnkd->bhrnqd", p.astype(v.dtype), vb)
    out = out.reshape(bt, h, dilation, lp, dh)[:, :, :, :L]
    out = out.transpose(0, 3, 2, 1, 4).reshape(bt, s, h, dh)
    lse = lse.reshape(bt, h, dilation, lp)[..., :L].transpose(0, 3, 2, 1).reshape(bt, s, h)
    return out, lse


def mixture_dilated_attention(q, k, v):
    slopes_all = alibi_slopes(N_HEADS)
    outs, lses = [], []
    for g, (window, dilation) in enumerate(DIL_GROUPS):
        sl = slice(g * HEADS_PER_GROUP, (g + 1) * HEADS_PER_GROUP)
        half = window // (2 * dilation)
        o, l = dilated_window_attention(q[:, :, sl], k[:, :, sl], v[:, :, sl],
                                        slopes_all[sl], dilation, half)
        outs.append(o)
        lses.append(l)
    w = jax.nn.softmax(jnp.stack(lses, axis=0), axis=0)
    o = jnp.sum(w[..., None] * jnp.stack(outs, axis=0).astype(jnp.float32), axis=0)
    bt, s = q.shape[0], q.shape[1]
    return o.reshape(bt, s, ATT_OUT).astype(q.dtype)


def setup_inputs(seed: int = 0) -> dict:
    key = jax.random.key(seed)
    ks = jax.random.split(key, 20)

    def nrm(k, shape, fan_in):
        return jax.random.normal(k, shape, jnp.float32) * (fan_in ** -0.5)

    def gain(k, shape):
        return 1.0 + 0.02 * jax.random.normal(k, shape, jnp.float32)

    def small(k, shape):
        return 0.02 * jax.random.normal(k, shape, jnp.float32)

    return {
        "x": jax.random.normal(ks[0], (BATCH, SEQ, D_MODEL), jnp.float32),
        "norm1_g": gain(ks[1], (DEPTH, D_MODEL)),
        "w_in": nrm(ks[2], (DEPTH, D_MODEL, N_IN), D_MODEL),
        "b_gate": small(ks[3], (DEPTH, N_BRANCH * D_MODEL)),
        "conv_a_w": nrm(ks[4], (DEPTH, CONV_K, CONV_CH), CONV_K),
        "conv_a_b": small(ks[5], (DEPTH, CONV_CH)),
        "ln_a_g": gain(ks[6], (DEPTH, CONV_CH)),
        "ln_a_b": small(ks[7], (DEPTH, CONV_CH)),
        "w_a_out": nrm(ks[8], (DEPTH, CONV_CH, D_MODEL), CONV_CH),
        "conv_b_w": nrm(ks[9], (DEPTH, SC_K, SC_CH), SC_K),
        "w_b_out": nrm(ks[10], (DEPTH, SC_CH, D_MODEL), SC_CH),
        "w_c_out": nrm(ks[11], (DEPTH, ATT_OUT, D_MODEL), ATT_OUT),
        "w_o": nrm(ks[12], (DEPTH, D_MODEL, D_MODEL), D_MODEL),
        "norm2_g": gain(ks[13], (DEPTH, D_MODEL)),
        "w_ffn_gate": nrm(ks[14], (DEPTH, D_MODEL, D_FF), D_MODEL),
        "w_ffn_up": nrm(ks[15], (DEPTH, D_MODEL, D_FF), D_MODEL),
        "w_ffn_down": nrm(ks[16], (DEPTH, D_FF, D_MODEL), D_FF),
        "final_g": gain(ks[17], (D_MODEL,)),
    }


def reference(x, norm1_g, w_in, b_gate, conv_a_w, conv_a_b, ln_a_g, ln_a_b, w_a_out,
              conv_b_w, w_b_out, w_c_out, w_o, norm2_g, w_ffn_gate, w_ffn_up, w_ffn_down,
              final_g):
    bt, s, _ = x.shape
    cuts = np.cumsum([2 * CONV_CH, 3 * SC_CH, ATT_W, ATT_W, ATT_W]).tolist()
    for l in range(DEPTH):
        h = rmsnorm(x, norm1_g[l])
        z = jnp.einsum("bsd,dn->bsn", h, w_in[l])
        za, zb, zq, zk, zv, zg = jnp.split(z, cuts, axis=-1)

        a = za[..., :CONV_CH] * jax.nn.sigmoid(za[..., CONV_CH:])
        a = depthwise_conv(a, conv_a_w[l], CONV_K // 2) + conv_a_b[l]
        a = jax.nn.silu(layernorm(a, ln_a_g[l], ln_a_b[l]))
        y_a = jnp.einsum("bsc,cd->bsd", a, w_a_out[l])

        gb, gc, u = jnp.split(zb, 3, axis=-1)
        y_b = jnp.einsum("bsc,cd->bsd", gb * depthwise_conv(gc * u, conv_b_w[l], SC_K // 2),
                         w_b_out[l])

        q = zq.reshape(bt, s, N_HEADS, HEAD_DIM)
        k = zk.reshape(bt, s, N_HEADS, HEAD_DIM)
        v = zv.reshape(bt, s, N_HEADS, HEAD_DIM)
        y_c = jnp.einsum("bsc,cd->bsd", mixture_dilated_attention(q, k, v), w_c_out[l])

        g_a, g_b, g_c = jnp.split(jax.nn.sigmoid(zg + b_gate[l]), N_BRANCH, axis=-1)
        m = g_a * y_a + g_b * y_b + g_c * y_c
        x = x + jnp.einsum("bsd,de->bse", m, w_o[l])

        h2 = rmsnorm(x, norm2_g[l])
        f = jax.nn.silu(jnp.einsum("bsd,df->bsf", h2, w_ffn_gate[l])) * \
            jnp.einsum("bsd,df->bsf", h2, w_ffn_up[l])
        x = x + jnp.einsum("bsf,fd->bsd", f, w_ffn_down[l])
    return rmsnorm(x, final_g)
```

```python
import functools

import numpy as np
import jax
import jax.numpy as jnp
from jax import lax
from jax.experimental import pallas as pl
from jax.experimental.pallas import tpu as pltpu

F32 = jnp.float32
BF16 = jnp.bfloat16

HEAD_DIM = 64
N_HEADS = 12
DIL_GROUPS = ((128, 1), (512, 4), (2048, 16))
HEADS_PER_GROUP = N_HEADS // len(DIL_GROUPS)
ATT_OUT = HEADS_PER_GROUP * HEAD_DIM
NORM_EPS = 1e-6
LN_EPS = 1e-5
NEG = -1e30

V7X_LANES = 128
V7X_BF16_SUBLANES = 16
V7X_VMEM_BYTES = 64 * 1024 * 1024
VMEM_LIMIT_BYTES = V7X_VMEM_BYTES - 8 * 1024 * 1024


def _params(*semantics):
    return pltpu.CompilerParams(dimension_semantics=semantics, vmem_limit_bytes=VMEM_LIMIT_BYTES)


def _tile(n, target):
    t = min(n, target)
    assert n % t == 0, (n, t)
    return t


def _rms_normed(x, g):
    ms = jnp.mean(x * x, axis=-1, keepdims=True)
    return x * lax.rsqrt(ms + NORM_EPS) * g


def _dot(a, b):
    return jnp.dot(a, b, preferred_element_type=F32)


def _in_proj_kernel(x_ref, g_ref, w_ref, a_ref, gb_ref, cu_ref, qkv_ref, *, ca, cb, aw):
    h = _rms_normed(x_ref[...], g_ref[...]).astype(BF16)

    def proj(c0, c1):
        return _dot(h, w_ref[:, c0:c1])

    a_ref[...] = (proj(0, ca) * jax.nn.sigmoid(proj(ca, 2 * ca))).astype(BF16)
    o = 2 * ca
    gb_ref[...] = proj(o, o + cb).astype(BF16)
    cu_ref[...] = (proj(o + cb, o + 2 * cb) * proj(o + 2 * cb, o + 3 * cb)).astype(BF16)
    o += 3 * cb
    qkv_ref[:, 0:aw] = (proj(o, o + aw) * (HEAD_DIM ** -0.5)).astype(BF16)
    qkv_ref[:, aw:3 * aw] = proj(o + aw, o + 3 * aw).astype(BF16)


def _in_proj(x2, g, w, ca, cb, aw):
    t, d = x2.shape
    n = w.shape[1]
    tm = _tile(t, 512)
    row = lambda i: (i, 0)
    fixed = lambda i: (0, 0)
    return pl.pallas_call(
        functools.partial(_in_proj_kernel, ca=ca, cb=cb, aw=aw),
        grid=(t // tm,),
        in_specs=[
            pl.BlockSpec((tm, d), row),
            pl.BlockSpec((1, d), fixed),
            pl.BlockSpec((d, n), fixed, pipeline_mode=pl.Buffered(1)),
        ],
        out_specs=[
            pl.BlockSpec((tm, ca), row),
            pl.BlockSpec((tm, cb), row),
            pl.BlockSpec((tm, cb), row),
            pl.BlockSpec((tm, 3 * aw), row),
        ],
        out_shape=[
            jax.ShapeDtypeStruct((t, ca), BF16),
            jax.ShapeDtypeStruct((t, cb), BF16),
            jax.ShapeDtypeStruct((t, cb), BF16),
            jax.ShapeDtypeStruct((t, 3 * aw), BF16),
        ],
        compiler_params=_params("parallel"),
        name="in_proj",
    )(x2, g, w)


CONV_HALO = V7X_BF16_SUBLANES
CONV_ROWS = 32


def _conv_kernel(a_c, a_p, a_n, c_c, c_p, c_n, gb_ref, wa_ref, ba_ref, lg_ref, lb_ref, wb_ref,
                 ao_ref, bo_ref, abuf, cbuf, *, ts, ka, kb):
    i = pl.program_id(1)
    last = pl.num_programs(1) - 1
    hl = CONV_HALO

    def fill(buf, cur, prev, nxt):
        buf[0:hl, :] = jnp.where(i > 0, prev[0].astype(F32), 0.0)
        buf[hl:hl + ts, :] = cur[0].astype(F32)
        buf[hl + ts:hl + ts + hl, :] = jnp.where(i < last, nxt[0].astype(F32), 0.0)

    fill(abuf, a_c, a_p, a_n)
    fill(cbuf, c_c, c_p, c_n)

    for r0 in range(0, ts, CONV_ROWS):
        rows = slice(r0, r0 + CONV_ROWS)
        acc = jnp.zeros((CONV_ROWS, abuf.shape[1]), F32) + ba_ref[...]
        for j in range(ka):
            s = hl + r0 + j - ka // 2
            acc = acc + wa_ref[j:j + 1, :] * abuf[s:s + CONV_ROWS, :]
        mu = jnp.mean(acc, axis=-1, keepdims=True)
        cen = acc - mu
        var = jnp.mean(cen * cen, axis=-1, keepdims=True)
        y = cen * lax.rsqrt(var + LN_EPS) * lg_ref[...] + lb_ref[...]
        ao_ref[0, rows, :] = (y * jax.nn.sigmoid(y)).astype(BF16)

        accb = jnp.zeros((CONV_ROWS, cbuf.shape[1]), F32)
        for j in range(kb):
            s = hl + r0 + j - kb // 2
            accb = accb + wb_ref[j:j + 1, :] * cbuf[s:s + CONV_ROWS, :]
        bo_ref[0, rows, :] = (gb_ref[0, rows, :].astype(F32) * accb).astype(BF16)


def _conv_branches(a, gb, cu, wa, ba, lg, lb, wb):
    bsz, s, ca = a.shape
    cb = cu.shape[2]
    ka, kb = wa.shape[0], wb.shape[0]
    ts = _tile(s, 512)
    hl = CONV_HALO
    assert ts % hl == 0 and ka // 2 < hl and kb // 2 < hl
    nh = s // hl
    cur = lambda b, i: (b, i, 0)
    prev = lambda b, i: (b, jnp.maximum(i * (ts // hl) - 1, 0), 0)
    nxt = lambda b, i: (b, jnp.minimum((i + 1) * (ts // hl), nh - 1), 0)
    fixed = lambda b, i: (0, 0)
    return pl.pallas_call(
        functools.partial(_conv_kernel, ts=ts, ka=ka, kb=kb),
        grid=(bsz, s // ts),
        in_specs=[
            pl.BlockSpec((1, ts, ca), cur), pl.BlockSpec((1, hl, ca), prev), pl.BlockSpec((1, hl, ca), nxt),
            pl.BlockSpec((1, ts, cb), cur), pl.BlockSpec((1, hl, cb), prev), pl.BlockSpec((1, hl, cb), nxt),
            pl.BlockSpec((1, ts, cb), cur),
            pl.BlockSpec((ka, ca), fixed), pl.BlockSpec((1, ca), fixed),
            pl.BlockSpec((1, ca), fixed), pl.BlockSpec((1, ca), fixed),
            pl.BlockSpec((kb, cb), fixed),
        ],
        out_specs=[pl.BlockSpec((1, ts, ca), cur), pl.BlockSpec((1, ts, cb), cur)],
        out_shape=[jax.ShapeDtypeStruct((bsz, s, ca), BF16), jax.ShapeDtypeStruct((bsz, s, cb), BF16)],
        scratch_shapes=[pltpu.VMEM((ts + 2 * hl, ca), F32), pltpu.VMEM((ts + 2 * hl, cb), F32)],
        compiler_params=_params("parallel", "parallel"),
        name="conv_branches",
    )(a, a, a, cu, cu, cu, gb, wa, ba, lg, lb, wb)


ATT_QROWS = 128


def _attn_kernel(q_ref, kc, kp, kn, vc, vp, vn, o_ref, lse_ref, kbuf, vbuf, *, tq, sub_len, dil, half, slopes):
    i = pl.program_id(2)
    kbuf[0:half, :] = kp[0]
    kbuf[half:half + tq, :] = kc[0]
    kbuf[half + tq:half + tq + half, :] = kn[0]
    vbuf[0:half, :] = vp[0]
    vbuf[half:half + tq, :] = vc[0]
    vbuf[half + tq:half + tq + half, :] = vn[0]

    qr = ATT_QROWS
    kw = qr + 2 * half
    lanes = V7X_LANES
    heads_per_pair = lanes // HEAD_DIM
    rel = (lax.broadcasted_iota(jnp.int32, (qr, kw), 1) - half
           - lax.broadcasted_iota(jnp.int32, (qr, kw), 0))
    arel = jnp.abs(rel)
    band = arel <= half
    dist = (dil * arel).astype(F32)
    lane_head = lax.broadcasted_iota(jnp.int32, (1, lanes), 1) // HEAD_DIM

    for o in range(0, tq, qr):
        kpos = i * tq + (o - half) + lax.broadcasted_iota(jnp.int32, (1, kw), 1)
        pen = jnp.where((kpos >= 0) & (kpos < sub_len), 0.0, NEG)
        for pair in range(q_ref.shape[2] // lanes):
            cols = slice(pair * lanes, (pair + 1) * lanes)
            qp = q_ref[0, o:o + qr, cols]
            kpair = kbuf[o:o + kw, cols]
            vpair = vbuf[o:o + kw, cols]
            out = None
            lse = None
            for hh in range(heads_per_pair):
                sel = lane_head == hh
                qm = jnp.where(sel, qp, jnp.zeros_like(qp))
                s = lax.dot_general(qm, kpair, (((1,), (1,)), ((), ())), preferred_element_type=F32)
                slope = slopes[pair * heads_per_pair + hh]
                s = s + jnp.where(band, -slope * dist, NEG) + pen
                m = jnp.max(s, axis=-1, keepdims=True)
                p = jnp.exp(s - m)
                l = jnp.sum(p, axis=-1, keepdims=True)
                pv = _dot(p.astype(BF16), vpair) / l
                hl = m + jnp.log(l)
                out = pv if out is None else jnp.where(sel, pv, out)
                lse = jnp.broadcast_to(hl, pv.shape) if lse is None else jnp.where(sel, hl, lse)
            o_ref[0, o:o + qr, cols] = out.astype(BF16)
            lse_ref[0, o:o + qr, cols] = lse


def _dilated_attention(qkv, g, aw):
    bsz, s, _ = qkv.shape
    window, dil = DIL_GROUPS[g]
    half = window // (2 * dil)
    sub_len = s // dil
    assert s % dil == 0 and sub_len % half == 0
    gw = ATT_OUT
    tq = _tile(sub_len, 512)
    assert tq % ATT_QROWS == 0 and tq % half == 0 and half % V7X_BF16_SUBLANES == 0
    nh = sub_len // half
    cpt = 3 * aw // gw
    slopes = tuple(float(2.0 ** (-8.0 * (g * HEADS_PER_GROUP + h + 1) / N_HEADS)) for h in range(HEADS_PER_GROUP))
    view = qkv.reshape(bsz, sub_len, dil * 3 * aw)

    def spec(rows, which, row_map):
        col0 = which * (aw // gw) + g
        return pl.BlockSpec((1, rows, gw), lambda b, c, i: (b, row_map(i), c * cpt + col0))

    cur = lambda i: i
    prev = lambda i: jnp.maximum(i * (tq // half) - 1, 0)
    nxt = lambda i: jnp.minimum((i + 1) * (tq // half), nh - 1)
    out_spec = pl.BlockSpec((1, tq, gw), lambda b, c, i: (b, i, c))
    o, lse = pl.pallas_call(
        functools.partial(_attn_kernel, tq=tq, sub_len=sub_len, dil=dil, half=half, slopes=slopes),
        grid=(bsz, dil, sub_len // tq),
        in_specs=[
            spec(tq, 0, cur),
            spec(tq, 1, cur), spec(half, 1, prev), spec(half, 1, nxt),
            spec(tq, 2, cur), spec(half, 2, prev), spec(half, 2, nxt),
        ],
        out_specs=[out_spec, out_spec],
        out_shape=[jax.ShapeDtypeStruct((bsz, sub_len, dil * gw), BF16),
                   jax.ShapeDtypeStruct((bsz, sub_len, dil * gw), F32)],
        scratch_shapes=[pltpu.VMEM((tq + 2 * half, gw), BF16), pltpu.VMEM((tq + 2 * half, gw), BF16)],
        compiler_params=_params("parallel", "parallel", "parallel"),
        name=f"dilated_attention_g{g}",
    )(view, view, view, view, view, view, view)
    return o.reshape(bsz * s, gw), lse.reshape(bsz * s, gw)


def _mixer_kernel(x_ref, g_ref, a_ref, b_ref, o0, o1, o2, l0, l1, l2, wa_ref, wb_ref, wc_ref,
                  wga_ref, wgb_ref, wgc_ref, bga_ref, bgb_ref, bgc_ref, wo_ref, out_ref, h_ref, att_ref):
    j = pl.program_id(1)

    @pl.when(j == 0)
    def _():
        x = x_ref[...]
        h_ref[...] = _rms_normed(x, g_ref[...]).astype(BF16)
        ls = (l0[...], l1[...], l2[...])
        m = jnp.maximum(jnp.maximum(ls[0], ls[1]), ls[2])
        es = [jnp.exp(l - m) for l in ls]
        num = es[0] * o0[...].astype(F32) + es[1] * o1[...].astype(F32) + es[2] * o2[...].astype(F32)
        att_ref[...] = (num / (es[0] + es[1] + es[2])).astype(BF16)
        out_ref[...] = x

    h = h_ref[...]
    mix = (jax.nn.sigmoid(_dot(h, wga_ref[...]) + bga_ref[...]) * _dot(a_ref[...], wa_ref[...])
           + jax.nn.sigmoid(_dot(h, wgb_ref[...]) + bgb_ref[...]) * _dot(b_ref[...], wb_ref[...])
           + jax.nn.sigmoid(_dot(h, wgc_ref[...]) + bgc_ref[...]) * _dot(att_ref[...], wc_ref[...]))
    out_ref[...] += _dot(mix.astype(BF16), wo_ref[...])


def _mixer(x2, g, a, b, os_, ls_, wa, wb, wc, wg, bg, wo):
    t, d = x2.shape
    ca, cb, gw = a.shape[1], b.shape[1], os_[0].shape[1]
    tm = _tile(t, 512)
    tn = _tile(d, 512)
    nj = d // tn
    row = lambda i, j: (i, 0)
    fixed = lambda i, j: (0, 0)
    col = lambda k: (lambda i, j: (0, k * nj + j))
    return pl.pallas_call(
        _mixer_kernel,
        grid=(t // tm, nj),
        in_specs=[
            pl.BlockSpec((tm, d), row), pl.BlockSpec((1, d), fixed),
            pl.BlockSpec((tm, ca), row), pl.BlockSpec((tm, cb), row),
            pl.BlockSpec((tm, gw), row), pl.BlockSpec((tm, gw), row), pl.BlockSpec((tm, gw), row),
            pl.BlockSpec((tm, gw), row), pl.BlockSpec((tm, gw), row), pl.BlockSpec((tm, gw), row),
            pl.BlockSpec((ca, tn), col(0)), pl.BlockSpec((cb, tn), col(0)), pl.BlockSpec((gw, tn), col(0)),
            pl.BlockSpec((d, tn), col(0)), pl.BlockSpec((d, tn), col(1)), pl.BlockSpec((d, tn), col(2)),
            pl.BlockSpec((1, tn), col(0)), pl.BlockSpec((1, tn), col(1)), pl.BlockSpec((1, tn), col(2)),
            pl.BlockSpec((tn, d), lambda i, j: (j, 0)),
        ],
        out_specs=pl.BlockSpec((tm, d), row),
        out_shape=jax.ShapeDtypeStruct((t, d), F32),
        scratch_shapes=[pltpu.VMEM((tm, d), BF16), pltpu.VMEM((tm, gw), BF16)],
        compiler_params=_params("parallel", "arbitrary"),
        name="gated_mixer",
    )(x2, g, a, b, *os_, *ls_, wa, wb, wc, wg, wg, wg, bg, bg, bg, wo)


def _ffn_kernel(x_ref, g_ref, wg_ref, wu_ref, wd_ref, *rest, final):
    if final:
        fg_ref, out_ref, h_ref = rest
    else:
        out_ref, h_ref = rest
    j = pl.program_id(1)

    @pl.when(j == 0)
    def _():
        x = x_ref[...]
        h_ref[...] = _rms_normed(x, g_ref[...]).astype(BF16)
        out_ref[...] = x

    h = h_ref[...]
    gate = _dot(h, wg_ref[...])
    f = gate * jax.nn.sigmoid(gate) * _dot(h, wu_ref[...])
    out_ref[...] += _dot(f.astype(BF16), wd_ref[...])

    if final:
        @pl.when(j == pl.num_programs(1) - 1)
        def _():
            out_ref[...] = _rms_normed(out_ref[...], fg_ref[...])


def _ffn(x2, g, wg, wu, wd, final_g=None):
    t, d = x2.shape
    dff = wg.shape[1]
    tm = _tile(t, 512)
    tf = _tile(dff, 512)
    row = lambda i, j: (i, 0)
    fixed = lambda i, j: (0, 0)
    final = final_g is not None
    in_specs = [
        pl.BlockSpec((tm, d), row), pl.BlockSpec((1, d), fixed),
        pl.BlockSpec((d, tf), lambda i, j: (0, j)), pl.BlockSpec((d, tf), lambda i, j: (0, j)),
        pl.BlockSpec((tf, d), lambda i, j: (j, 0)),
    ]
    args = [x2, g, wg, wu, wd]
    if final:
        in_specs.append(pl.BlockSpec((1, d), fixed))
        args.append(final_g)
    return pl.pallas_call(
        functools.partial(_ffn_kernel, final=final),
        grid=(t // tm, dff // tf),
        in_specs=in_specs,
        out_specs=pl.BlockSpec((tm, d), row),
        out_shape=jax.ShapeDtypeStruct((t, d), F32),
        scratch_shapes=[pltpu.VMEM((tm, d), BF16)],
        compiler_params=_params("parallel", "arbitrary"),
        name="swiglu_ffn_final" if final else "swiglu_ffn",
    )(*args)


def kernel(x, norm1_g, w_in, b_gate, conv_a_w, conv_a_b, ln_a_g, ln_a_b, w_a_out, conv_b_w, w_b_out,
           w_c_out, w_o, norm2_g, w_ffn_gate, w_ffn_up, w_ffn_down, final_g):
    bsz, s, d = x.shape
    depth = w_in.shape[0]
    ca, cb = conv_a_w.shape[2], conv_b_w.shape[2]
    n_proj = w_in.shape[2] - 3 * d
    aw = (n_proj - 2 * ca - 3 * cb) // 3
    assert aw == N_HEADS * HEAD_DIM and w_c_out.shape[1] == ATT_OUT

    row = lambda v: v.reshape(1, -1)
    x2 = x.reshape(bsz * s, d)
    for l in range(depth):
        w_proj = w_in[l, :, :n_proj].astype(BF16)
        w_gate = w_in[l, :, n_proj:].astype(BF16)
        a, gb, cu, qkv = _in_proj(x2, row(norm1_g[l]), w_proj, ca, cb, aw)
        a_act, b_act = _conv_branches(
            a.reshape(bsz, s, ca), gb.reshape(bsz, s, cb), cu.reshape(bsz, s, cb),
            conv_a_w[l], row(conv_a_b[l]), row(ln_a_g[l]), row(ln_a_b[l]), conv_b_w[l])
        att = [_dilated_attention(qkv.reshape(bsz, s, 3 * aw), g, aw) for g in range(len(DIL_GROUPS))]
        x2 = _mixer(x2, row(norm1_g[l]), a_act.reshape(bsz * s, ca), b_act.reshape(bsz * s, cb),
                    [o for o, _ in att], [ls for _, ls in att],
                    w_a_out[l].astype(BF16), w_b_out[l].astype(BF16), w_c_out[l].astype(BF16),
                    w_gate, row(b_gate[l]), w_o[l].astype(BF16))
        x2 = _ffn(x2, row(norm2_g[l]), w_ffn_gate[l].astype(BF16), w_ffn_up[l].astype(BF16),
                  w_ffn_down[l].astype(BF16), row(final_g) if l == depth - 1 else None)
    return x2.reshape(bsz, s, d)
```

```python
import functools

import numpy as np
import jax
import jax.numpy as jnp
from jax import lax
from jax.experimental import pallas as pl
from jax.experimental.pallas import tpu as pltpu

F32 = jnp.float32
BF16 = jnp.bfloat16

HEAD_DIM = 64
N_HEADS = 12
DIL_GROUPS = ((128, 1), (512, 4), (2048, 16))
HEADS_PER_GROUP = N_HEADS // len(DIL_GROUPS)
ATT_OUT = HEADS_PER_GROUP * HEAD_DIM
NORM_EPS = 1e-6
LN_EPS = 1e-5
NEG = -1e30

V7X_LANES = 128
V7X_F32_SUBLANES = 8
V7X_BF16_SUBLANES = 16
V7X_VMEM_BYTES = 64 * 1024 * 1024
VMEM_LIMIT_BYTES = V7X_VMEM_BYTES - 8 * 1024 * 1024


def _params(*semantics):
    return pltpu.CompilerParams(dimension_semantics=semantics, vmem_limit_bytes=VMEM_LIMIT_BYTES)


def _tile(n, target):
    t = min(n, target)
    assert n % t == 0, (n, t)
    return t


def _rms_normed(x, g):
    ms = jnp.mean(x * x, axis=-1, keepdims=True)
    return x * lax.rsqrt(ms + NORM_EPS) * g


def _dot(a, b):
    return jnp.dot(a, b, preferred_element_type=F32)


def _in_proj_kernel(x_ref, g_ref, w_ref, a_ref, gb_ref, cu_ref, *rest, ca, cb, gw, dils):
    ng = len(dils)
    qkv_refs, zs = rest[:ng], rest[ng]
    h = _rms_normed(x_ref[...], g_ref[...]).astype(BF16)
    tm = h.shape[0]

    def proj(c0, c1):
        return _dot(h, w_ref[:, c0:c1])

    a_ref[...] = (proj(0, ca) * jax.nn.sigmoid(proj(ca, 2 * ca))).astype(BF16)
    o = 2 * ca
    gb_ref[...] = proj(o, o + cb).astype(BF16)
    cu_ref[...] = (proj(o + cb, o + 2 * cb) * proj(o + 2 * cb, o + 3 * cb)).astype(BF16)
    o += 3 * cb
    lanes = V7X_LANES
    for g, dil in enumerate(dils):
        q = proj(o, o + gw) * (HEAD_DIM ** -0.5)
        kv = proj(o + gw, o + 3 * gw)
        o += 3 * gw
        ref = qkv_refs[g]
        if dil == 1:
            ref[:, 0:gw] = q.astype(BF16)
            ref[:, gw:3 * gw] = kv.astype(BF16)
            continue
        for s in range(gw // lanes):
            zs[s] = q[:, s * lanes:(s + 1) * lanes]
        for s in range(2 * gw // lanes):
            zs[gw // lanes + s] = kv[:, s * lanes:(s + 1) * lanes]
        for c in range(dil):
            for s in range(3 * gw // lanes):
                ref[0, c, :, s * lanes:(s + 1) * lanes] = zs[s, pl.ds(c, tm // dil, stride=dil), :].astype(BF16)


def _in_proj(x2, g, w, ca, cb, gw, bsz):
    t, d = x2.shape
    n = w.shape[1]
    s = t // bsz
    dils = tuple(dil for _, dil in DIL_GROUPS)
    tm = _tile(s, 512)
    nt = s // tm
    assert all(tm % (dil * V7X_BF16_SUBLANES) == 0 for dil in dils if dil > 1)
    row = lambda i: (i, 0)
    fixed = lambda i: (0, 0)
    qkv_specs, qkv_shapes = [], []
    for dil in dils:
        if dil == 1:
            qkv_specs.append(pl.BlockSpec((tm, 3 * gw), row))
            qkv_shapes.append(jax.ShapeDtypeStruct((t, 3 * gw), BF16))
        else:
            qkv_specs.append(pl.BlockSpec((1, dil, tm // dil, 3 * gw), lambda i: (i // nt, 0, i % nt, 0)))
            qkv_shapes.append(jax.ShapeDtypeStruct((bsz, dil, s // dil, 3 * gw), BF16))
    return pl.pallas_call(
        functools.partial(_in_proj_kernel, ca=ca, cb=cb, gw=gw, dils=dils),
        grid=(t // tm,),
        in_specs=[
            pl.BlockSpec((tm, d), row),
            pl.BlockSpec((1, d), fixed),
            pl.BlockSpec((d, n), fixed, pipeline_mode=pl.Buffered(1)),
        ],
        out_specs=[
            pl.BlockSpec((tm, ca), row),
            pl.BlockSpec((tm, cb), row),
            pl.BlockSpec((tm, cb), row),
        ] + qkv_specs,
        out_shape=[
            jax.ShapeDtypeStruct((t, ca), BF16),
            jax.ShapeDtypeStruct((t, cb), BF16),
            jax.ShapeDtypeStruct((t, cb), BF16),
        ] + qkv_shapes,
        scratch_shapes=[pltpu.VMEM((3 * gw // V7X_LANES, tm, V7X_LANES), F32)],
        compiler_params=_params("parallel"),
        name="in_proj",
    )(x2, g, w)


CONV_HALO = V7X_BF16_SUBLANES
CONV_ROWS = 32


def _conv_kernel(a_c, a_p, a_n, c_c, c_p, c_n, gb_ref, wa_ref, ba_ref, lg_ref, lb_ref, wb_ref,
                 ao_ref, bo_ref, abuf, cbuf, *, ts, ka, kb):
    i = pl.program_id(1)
    last = pl.num_programs(1) - 1
    hl = CONV_HALO
    sub = V7X_F32_SUBLANES

    def fill(buf, cur, prev, nxt):
        buf[0:hl, :] = jnp.where(i > 0, prev[0].astype(F32), 0.0)
        buf[hl:hl + ts, :] = cur[0].astype(F32)
        buf[hl + ts:hl + ts + hl, :] = jnp.where(i < last, nxt[0].astype(F32), 0.0)

    fill(abuf.at[0], a_c, a_p, a_n)
    fill(cbuf, c_c, c_p, c_n)
    n_sh = abuf.shape[1] - sub
    for s in range(1, sub):
        abuf[s, 0:n_sh, :] = abuf[0, s:s + n_sh, :]

    for r0 in range(0, ts, CONV_ROWS):
        rows = slice(r0, r0 + CONV_ROWS)
        acc = jnp.zeros((CONV_ROWS // sub, sub, abuf.shape[2]), F32)
        for j in range(ka):
            s = hl + r0 + j - ka // 2
            al = s // sub * sub
            acc = acc + wa_ref[j] * abuf[s % sub, al:al + CONV_ROWS, :].reshape(acc.shape)
        acc = acc.reshape(CONV_ROWS, abuf.shape[2]) + ba_ref[...]
        mu = jnp.mean(acc, axis=-1, keepdims=True)
        cen = acc - mu
        var = jnp.mean(cen * cen, axis=-1, keepdims=True)
        y = cen * lax.rsqrt(var + LN_EPS) * lg_ref[...] + lb_ref[...]
        ao_ref[0, rows, :] = (y * jax.nn.sigmoid(y)).astype(BF16)

        accb = jnp.zeros((CONV_ROWS, cbuf.shape[1]), F32)
        for j in range(kb):
            s = hl + r0 + j - kb // 2
            accb = accb + wb_ref[j:j + 1, :] * cbuf[s:s + CONV_ROWS, :]
        bo_ref[0, rows, :] = (gb_ref[0, rows, :].astype(F32) * accb).astype(BF16)


def _conv_branches(a, gb, cu, wa, ba, lg, lb, wb):
    bsz, s, ca = a.shape
    cb = cu.shape[2]
    ka, kb = wa.shape[0], wb.shape[0]
    ts = _tile(s, 512)
    hl = CONV_HALO
    assert ts % hl == 0 and ka // 2 < hl and kb // 2 < hl
    nh = s // hl
    cur = lambda b, i: (b, i, 0)
    prev = lambda b, i: (b, jnp.maximum(i * (ts // hl) - 1, 0), 0)
    nxt = lambda b, i: (b, jnp.minimum((i + 1) * (ts // hl), nh - 1), 0)
    fixed = lambda b, i: (0, 0)
    return pl.pallas_call(
        functools.partial(_conv_kernel, ts=ts, ka=ka, kb=kb),
        grid=(bsz, s // ts),
        in_specs=[
            pl.BlockSpec((1, ts, ca), cur), pl.BlockSpec((1, hl, ca), prev), pl.BlockSpec((1, hl, ca), nxt),
            pl.BlockSpec((1, ts, cb), cur), pl.BlockSpec((1, hl, cb), prev), pl.BlockSpec((1, hl, cb), nxt),
            pl.BlockSpec((1, ts, cb), cur),
            pl.BlockSpec((ka, V7X_F32_SUBLANES, ca), lambda b, i: (0, 0, 0)), pl.BlockSpec((1, ca), fixed),
            pl.BlockSpec((1, ca), fixed), pl.BlockSpec((1, ca), fixed),
            pl.BlockSpec((kb, cb), fixed),
        ],
        out_specs=[pl.BlockSpec((1, ts, ca), cur), pl.BlockSpec((1, ts, cb), cur)],
        out_shape=[jax.ShapeDtypeStruct((bsz, s, ca), BF16), jax.ShapeDtypeStruct((bsz, s, cb), BF16)],
        scratch_shapes=[pltpu.VMEM((V7X_F32_SUBLANES, ts + 2 * hl, ca), F32), pltpu.VMEM((ts + 2 * hl, cb), F32)],
        compiler_params=_params("parallel", "parallel"),
        name="conv_branches",
    )(a, a, a, cu, cu, cu, gb, jnp.broadcast_to(wa[:, None, :], (ka, V7X_F32_SUBLANES, ca)), ba, lg, lb, wb)


ATT_QROWS = 128


def _attn_kernel(q_ref, kc, kp, kn, vc, vp, vn, o_ref, lse_ref, kbuf, vbuf, *, tq, sub_len, dil, half, slopes):
    i = pl.program_id(2)
    kbuf[0:half, :] = kp[0]
    kbuf[half:half + tq, :] = kc[0]
    kbuf[half + tq:half + tq + half, :] = kn[0]
    vbuf[0:half, :] = vp[0]
    vbuf[half:half + tq, :] = vc[0]
    vbuf[half + tq:half + tq + half, :] = vn[0]

    qr = ATT_QROWS
    kw = qr + 2 * half
    lanes = V7X_LANES
    heads_per_pair = lanes // HEAD_DIM
    rel = (lax.broadcasted_iota(jnp.int32, (qr, kw), 1) - half
           - lax.broadcasted_iota(jnp.int32, (qr, kw), 0))
    arel = jnp.abs(rel)
    band = arel <= half
    dist = (dil * arel).astype(F32)
    lane_head = lax.broadcasted_iota(jnp.int32, (1, lanes), 1) // HEAD_DIM

    for o in range(0, tq, qr):
        kpos = i * tq + (o - half) + lax.broadcasted_iota(jnp.int32, (1, kw), 1)
        pen = jnp.where((kpos >= 0) & (kpos < sub_len), 0.0, NEG)
        for pair in range(q_ref.shape[2] // lanes):
            cols = slice(pair * lanes, (pair + 1) * lanes)
            qp = q_ref[0, o:o + qr, cols]
            kpair = kbuf[o:o + kw, cols]
            vpair = vbuf[o:o + kw, cols]
            out = None
            lse = None
            for hh in range(heads_per_pair):
                sel = lane_head == hh
                qm = jnp.where(sel, qp, jnp.zeros_like(qp))
                s = lax.dot_general(qm, kpair, (((1,), (1,)), ((), ())), preferred_element_type=F32)
                slope = slopes[pair * heads_per_pair + hh]
                s = s + jnp.where(band, -slope * dist, NEG) + pen
                m = jnp.max(s, axis=-1, keepdims=True)
                p = jnp.exp(s - m)
                l = jnp.sum(p, axis=-1, keepdims=True)
                pv = _dot(p.astype(BF16), vpair) / l
                hl = m + jnp.log(l)
                out = pv if out is None else jnp.where(sel, pv, out)
                lse = jnp.broadcast_to(hl, pv.shape) if lse is None else jnp.where(sel, hl, lse)
            o_ref[0, o:o + qr, cols] = out.astype(BF16)
            lse_ref[0, o:o + qr, cols] = lse


def _dilated_attention(qkv, g):
    bsz, dil, sub_len, w3 = qkv.shape
    window, gdil = DIL_GROUPS[g]
    assert gdil == dil
    half = window // (2 * dil)
    gw = w3 // 3
    assert sub_len % half == 0
    tq = _tile(sub_len, 512)
    assert tq % ATT_QROWS == 0 and tq % half == 0 and half % V7X_BF16_SUBLANES == 0
    nh = sub_len // half
    slopes = tuple(float(2.0 ** (-8.0 * (g * HEADS_PER_GROUP + h + 1) / N_HEADS)) for h in range(HEADS_PER_GROUP))

    def spec(rows, which, row_map):
        return pl.BlockSpec((1, None, rows, gw), lambda b, c, i: (b, c, row_map(i), which))

    cur = lambda i: i
    prev = lambda i: jnp.maximum(i * (tq // half) - 1, 0)
    nxt = lambda i: jnp.minimum((i + 1) * (tq // half), nh - 1)
    out_spec = pl.BlockSpec((1, None, tq, gw), lambda b, c, i: (b, c, i, 0))
    return pl.pallas_call(
        functools.partial(_attn_kernel, tq=tq, sub_len=sub_len, dil=dil, half=half, slopes=slopes),
        grid=(bsz, dil, sub_len // tq),
        in_specs=[
            spec(tq, 0, cur),
            spec(tq, 1, cur), spec(half, 1, prev), spec(half, 1, nxt),
            spec(tq, 2, cur), spec(half, 2, prev), spec(half, 2, nxt),
        ],
        out_specs=[out_spec, out_spec],
        out_shape=[jax.ShapeDtypeStruct((bsz, dil, sub_len, gw), BF16),
                   jax.ShapeDtypeStruct((bsz, dil, sub_len, gw), F32)],
        scratch_shapes=[pltpu.VMEM((tq + 2 * half, gw), BF16), pltpu.VMEM((tq + 2 * half, gw), BF16)],
        compiler_params=_params("parallel", "parallel", "parallel"),
        name=f"dilated_attention_g{g}",
    )(qkv, qkv, qkv, qkv, qkv, qkv, qkv)


def _natural_order(ref, nat_ref):
    _, dil, rows, w = ref.shape
    if dil == 1:
        return ref[0, 0].astype(F32)
    lanes = V7X_LANES
    for c in range(dil):
        for s in range(w // lanes):
            nat_ref[s, pl.ds(c, rows, stride=dil), :] = ref[0, c, :, s * lanes:(s + 1) * lanes].astype(F32)
    return jnp.concatenate([nat_ref[s] for s in range(w // lanes)], axis=1)


def _mixer_kernel(x_ref, g_ref, a_ref, b_ref, o0, o1, o2, l0, l1, l2, wa_ref, wb_ref, wc_ref,
                  wga_ref, wgb_ref, wgc_ref, bga_ref, bgb_ref, bgc_ref, wo_ref, out_ref, h_ref, att_ref, nat_ref):
    j = pl.program_id(1)

    @pl.when(j == 0)
    def _():
        x = x_ref[...]
        h_ref[...] = _rms_normed(x, g_ref[...]).astype(BF16)
        ls = [_natural_order(l, nat_ref) for l in (l0, l1, l2)]
        m = jnp.maximum(jnp.maximum(ls[0], ls[1]), ls[2])
        es = [jnp.exp(l - m) for l in ls]
        num = None
        for e, o in zip(es, (o0, o1, o2)):
            term = e * _natural_order(o, nat_ref)
            num = term if num is None else num + term
        att_ref[...] = (num / (es[0] + es[1] + es[2])).astype(BF16)
        out_ref[...] = x

    h = h_ref[...]
    mix = (jax.nn.sigmoid(_dot(h, wga_ref[...]) + bga_ref[...]) * _dot(a_ref[...], wa_ref[...])
           + jax.nn.sigmoid(_dot(h, wgb_ref[...]) + bgb_ref[...]) * _dot(b_ref[...], wb_ref[...])
           + jax.nn.sigmoid(_dot(h, wgc_ref[...]) + bgc_ref[...]) * _dot(att_ref[...], wc_ref[...]))
    out_ref[...] += _dot(mix.astype(BF16), wo_ref[...])


def _mixer(x2, g, a, b, os_, ls_, wa, wb, wc, wg, bg, wo):
    t, d = x2.shape
    ca, cb = a.shape[1], b.shape[1]
    bsz, _, _, gw = os_[0].shape
    s = t // bsz
    tm = _tile(s, 512)
    nt = s // tm
    tn = _tile(d, 512)
    nj = d // tn
    row = lambda i, j: (i, 0)
    fixed = lambda i, j: (0, 0)
    col = lambda k: (lambda i, j: (0, k * nj + j))
    att_specs = [pl.BlockSpec((1, o.shape[1], tm // o.shape[1], gw), lambda i, j: (i // nt, 0, i % nt, 0))
                 for o in os_]
    return pl.pallas_call(
        _mixer_kernel,
        grid=(t // tm, nj),
        in_specs=[
            pl.BlockSpec((tm, d), row), pl.BlockSpec((1, d), fixed),
            pl.BlockSpec((tm, ca), row), pl.BlockSpec((tm, cb), row),
            *att_specs, *att_specs,
            pl.BlockSpec((ca, tn), col(0)), pl.BlockSpec((cb, tn), col(0)), pl.BlockSpec((gw, tn), col(0)),
            pl.BlockSpec((d, tn), col(0)), pl.BlockSpec((d, tn), col(1)), pl.BlockSpec((d, tn), col(2)),
            pl.BlockSpec((1, tn), col(0)), pl.BlockSpec((1, tn), col(1)), pl.BlockSpec((1, tn), col(2)),
            pl.BlockSpec((tn, d), lambda i, j: (j, 0)),
        ],
        out_specs=pl.BlockSpec((tm, d), row),
        out_shape=jax.ShapeDtypeStruct((t, d), F32),
        scratch_shapes=[pltpu.VMEM((tm, d), BF16), pltpu.VMEM((tm, gw), BF16),
                        pltpu.VMEM((gw // V7X_LANES, tm, V7X_LANES), F32)],
        compiler_params=_params("parallel", "arbitrary"),
        name="gated_mixer",
    )(x2, g, a, b, *os_, *ls_, wa, wb, wc, wg, wg, wg, bg, bg, bg, wo)


def _ffn_kernel(x_ref, g_ref, wg_ref, wu_ref, wd_ref, *rest, final):
    if final:
        fg_ref, out_ref, h_ref = rest
    else:
        out_ref, h_ref = rest
    j = pl.program_id(1)

    @pl.when(j == 0)
    def _():
        x = x_ref[...]
        h_ref[...] = _rms_normed(x, g_ref[...]).astype(BF16)
        out_ref[...] = x

    h = h_ref[...]
    gate = _dot(h, wg_ref[...])
    f = gate * jax.nn.sigmoid(gate) * _dot(h, wu_ref[...])
    out_ref[...] += _dot(f.astype(BF16), wd_ref[...])

    if final:
        @pl.when(j == pl.num_programs(1) - 1)
        def _():
            out_ref[...] = _rms_normed(out_ref[...], fg_ref[...])


def _ffn(x2, g, wg, wu, wd, final_g=None):
    t, d = x2.shape
    dff = wg.shape[1]
    tm = _tile(t, 1024)
    tf = _tile(dff, 512)
    row = lambda i, j: (i, 0)
    fixed = lambda i, j: (0, 0)
    final = final_g is not None
    in_specs = [
        pl.BlockSpec((tm, d), row), pl.BlockSpec((1, d), fixed),
        pl.BlockSpec((d, tf), lambda i, j: (0, j)), pl.BlockSpec((d, tf), lambda i, j: (0, j)),
        pl.BlockSpec((tf, d), lambda i, j: (j, 0)),
    ]
    args = [x2, g, wg, wu, wd]
    if final:
        in_specs.append(pl.BlockSpec((1, d), fixed))
        args.append(final_g)
    return pl.pallas_call(
        functools.partial(_ffn_kernel, final=final),
        grid=(t // tm, dff // tf),
        in_specs=in_specs,
        out_specs=pl.BlockSpec((tm, d), row),
        out_shape=jax.ShapeDtypeStruct((t, d), F32),
        scratch_shapes=[pltpu.VMEM((tm, d), BF16)],
        compiler_params=_params("parallel", "arbitrary"),
        name="swiglu_ffn_final" if final else "swiglu_ffn",
    )(*args)


def kernel(x, norm1_g, w_in, b_gate, conv_a_w, conv_a_b, ln_a_g, ln_a_b, w_a_out, conv_b_w, w_b_out,
           w_c_out, w_o, norm2_g, w_ffn_gate, w_ffn_up, w_ffn_down, final_g):
    bsz, s, d = x.shape
    depth = w_in.shape[0]
    ca, cb = conv_a_w.shape[2], conv_b_w.shape[2]
    n_proj = w_in.shape[2] - 3 * d
    aw = (n_proj - 2 * ca - 3 * cb) // 3
    assert aw == N_HEADS * HEAD_DIM and w_c_out.shape[1] == ATT_OUT

    row = lambda v: v.reshape(1, -1)
    x2 = x.reshape(bsz * s, d)
    n_ab = 2 * ca + 3 * cb
    gw = ATT_OUT
    for l in range(depth):
        w_qkv = w_in[l, :, n_ab:n_proj].reshape(d, 3, len(DIL_GROUPS), gw)
        w_proj = jnp.concatenate(
            [w_in[l, :, :n_ab], w_qkv.transpose(0, 2, 1, 3).reshape(d, 3 * aw)], axis=1).astype(BF16)
        w_gate = w_in[l, :, n_proj:].astype(BF16)
        a, gb, cu, *qkvs = _in_proj(x2, row(norm1_g[l]), w_proj, ca, cb, gw, bsz)
        a_act, b_act = _conv_branches(
            a.reshape(bsz, s, ca), gb.reshape(bsz, s, cb), cu.reshape(bsz, s, cb),
            conv_a_w[l], row(conv_a_b[l]), row(ln_a_g[l]), row(ln_a_b[l]), conv_b_w[l])
        att = [_dilated_attention(qkv.reshape(bsz, dil, s // dil, 3 * gw), g)
               for g, (qkv, (_, dil)) in enumerate(zip(qkvs, DIL_GROUPS))]
        x2 = _mixer(x2, row(norm1_g[l]), a_act.reshape(bsz * s, ca), b_act.reshape(bsz * s, cb),
                    [o for o, _ in att], [ls for _, ls in att],
                    w_a_out[l].astype(BF16), w_b_out[l].astype(BF16), w_c_out[l].astype(BF16),
                    w_gate, row(b_gate[l]), w_o[l].astype(BF16))
        x2 = _ffn(x2, row(norm2_g[l]), w_ffn_gate[l].astype(BF16), w_ffn_up[l].astype(BF16),
                  w_ffn_down[l].astype(BF16), row(final_g) if l == depth - 1 else None)
    return x2.reshape(bsz, s, d)
```

```python
import functools

import jax
import jax.numpy as jnp
from jax import lax
from jax.experimental import pallas as pl
from jax.experimental.pallas import tpu as pltpu

F32 = jnp.float32
BF16 = jnp.bfloat16

HEAD_DIM = 64
N_HEADS = 12
DIL_GROUPS = ((128, 1), (512, 4), (2048, 16))
HEADS_PER_GROUP = N_HEADS // len(DIL_GROUPS)
ATT_OUT = HEADS_PER_GROUP * HEAD_DIM
NORM_EPS = 1e-6
LN_EPS = 1e-5
NEG = -1e30

V7X_LANES = 128
V7X_F32_SUBLANES = 8
V7X_BF16_SUBLANES = 16
V7X_VMEM_BYTES = 64 * 1024 * 1024
VMEM_LIMIT_BYTES = V7X_VMEM_BYTES - 8 * 1024 * 1024


def _params(*semantics):
    return pltpu.CompilerParams(dimension_semantics=semantics, vmem_limit_bytes=VMEM_LIMIT_BYTES)


def _tile(n, target):
    t = min(n, target)
    assert n % t == 0, (n, t)
    return t


def _rms_normed(x, g):
    ms = jnp.mean(x * x, axis=-1, keepdims=True)
    return x * lax.rsqrt(ms + NORM_EPS) * g


def _dot(a, b):
    return jnp.dot(a, b, preferred_element_type=F32)


CAST_BLOCK_BYTES = 4 * 1024 * 1024

def _cast_kernel(w_ref, *out_refs):
    c0 = 0
    for o_ref in out_refs:
        c1 = c0 + o_ref.shape[-1]
        o_ref[...] = w_ref[:, :, c0:c1].astype(BF16)
        c0 = c1


def _cast_bf16(w, splits=None):
    nl, r, c = w.shape
    splits = (c,) if splits is None else splits
    assert sum(splits) == c
    rb = r
    while rb * c * 4 > CAST_BLOCK_BYTES and rb % (2 * V7X_BF16_SUBLANES) == 0:
        rb //= 2
    idx = lambda l, i: (l, i, 0)
    outs = pl.pallas_call(
        _cast_kernel,
        grid=(nl, r // rb),
        in_specs=[pl.BlockSpec((1, rb, c), idx)],
        out_specs=[pl.BlockSpec((1, rb, n), idx) for n in splits],
        out_shape=[jax.ShapeDtypeStruct((nl, r, n), BF16) for n in splits],
        compiler_params=_params("parallel", "parallel"),
        name="cast_bf16",
    )(w)
    return outs[0] if len(splits) == 1 else outs


CONV_HALO = V7X_BF16_SUBLANES
CONV_ROWS = 32


def _shifted_copies(abuf):
    sub = V7X_F32_SUBLANES
    n_sh = abuf.shape[1] - sub
    for s in range(1, sub):
        abuf[s, 0:n_sh, :] = abuf[0, s:s + n_sh, :]


def _conv_a_chunk(abuf, wa_ref, ba_ref, lg_ref, lb_ref, out_ref, r0, out_row0, ka):
    sub = V7X_F32_SUBLANES
    acc = jnp.zeros((CONV_ROWS // sub, sub, abuf.shape[2]), F32)
    for j in range(ka):
        s = CONV_HALO + r0 + j - ka // 2
        al = s // sub * sub
        acc = acc + wa_ref[j] * abuf[s % sub, al:al + CONV_ROWS, :].reshape(acc.shape)
    acc = acc.reshape(CONV_ROWS, abuf.shape[2]) + ba_ref[...]
    mu = jnp.mean(acc, axis=-1, keepdims=True)
    cen = acc - mu
    var = jnp.mean(cen * cen, axis=-1, keepdims=True)
    y = cen * lax.rsqrt(var + LN_EPS) * lg_ref[...] + lb_ref[...]
    out_ref[out_row0 + r0:out_row0 + r0 + CONV_ROWS, :] = (y * jax.nn.sigmoid(y)).astype(BF16)


def _in_proj_kernel(x_ref, g_ref, w_ref, wa_ref, ba_ref, lg_ref, lb_ref, wb_ref,
                    h_ref, aact_ref, bact_ref, *rest, ca, cb, aw, gw, dils, nt, ka, kb):
    ng = len(dils)
    qkv_refs = rest[:ng]
    zs, a_tail, a_cur, c_tail, c_cur, gb_cur, abuf, cbuf = rest[ng:]
    i = pl.program_id(0)
    tm = x_ref.shape[0]
    hl = CONV_HALO
    lanes = V7X_LANES

    @pl.when(i == 0)
    def _():
        for ref in (a_tail, a_cur, c_tail, c_cur, gb_cur):
            ref[...] = jnp.zeros(ref.shape, ref.dtype)

    h = _rms_normed(x_ref[...], g_ref[...]).astype(BF16)
    h_ref[...] = h
    new = {}

    def proj(c0, c1):
        return _dot(h, w_ref[:, c0:c1])

    def glu():
        new["a"] = (proj(0, ca) * jax.nn.sigmoid(proj(ca, 2 * ca))).astype(BF16)

    def gate_b():
        new["gb"] = proj(2 * ca, 2 * ca + cb).astype(BF16)

    def conv_b_in():
        o = 2 * ca + cb
        new["cu"] = (proj(o, o + cb) * proj(o + cb, o + 2 * cb)).astype(BF16)

    def qkv(g, n):
        def run():
            o = 2 * ca + 3 * cb + n * aw + g * gw
            z = proj(o, o + gw)
            if n == 0:
                z = z * (HEAD_DIM ** -0.5)
            ref, dil = qkv_refs[g], dils[g]
            if dil == 1:
                ref[:, n * gw:(n + 1) * gw] = z.astype(BF16)
                return
            for s in range(gw // lanes):
                zs[s] = z[:, s * lanes:(s + 1) * lanes]
            for c in range(dil):
                for s in range(gw // lanes):
                    ref[0, c, :, n * gw + s * lanes:n * gw + (s + 1) * lanes] = (
                        zs[s, pl.ds(c, tm // dil, stride=dil), :].astype(BF16))
        return run

    mxu_tasks = [glu, gate_b, conv_b_in] + [qkv(g, n) for g in range(ng) for n in range(3)]

    p = lax.rem(i + nt - 1, nt)
    first, last = p == 0, p == nt - 1
    half_rows = tm // 2
    span = half_rows + 2 * hl

    def fill_first_half():
        abuf[0, 0:hl, :] = jnp.where(first, 0.0, a_tail[...].astype(F32))
        abuf[0, hl:span, :] = a_cur[0:span - hl, :].astype(F32)
        _shifted_copies(abuf)

    def fill_second_half():
        abuf[0, 0:span - hl, :] = a_cur[half_rows - hl:tm, :].astype(F32)
        abuf[0, span - hl:span, :] = jnp.where(last, 0.0, new["a"][0:hl, :].astype(F32))
        _shifted_copies(abuf)

    def conv_a(r0, row0):
        return lambda: _conv_a_chunk(abuf, wa_ref, ba_ref, lg_ref, lb_ref, aact_ref, r0, row0, ka)

    def fill_b():
        cbuf[0:hl, :] = jnp.where(first, 0.0, c_tail[...].astype(F32))
        cbuf[hl:hl + tm, :] = c_cur[...].astype(F32)
        cbuf[hl + tm:hl + tm + hl, :] = jnp.where(last, 0.0, new["cu"][0:hl, :].astype(F32))

    def conv_b(rows0, rows1):
        def run():
            for r0 in range(rows0, rows1, CONV_ROWS):
                accb = jnp.zeros((CONV_ROWS, cb), F32)
                for j in range(kb):
                    s = hl + r0 + j - kb // 2
                    accb = accb + wb_ref[j:j + 1, :] * cbuf[s:s + CONV_ROWS, :]
                bact_ref[r0:r0 + CONV_ROWS, :] = (gb_cur[r0:r0 + CONV_ROWS, :].astype(F32) * accb).astype(BF16)
        return run

    vpu_tasks = ([fill_first_half] + [conv_a(r0, 0) for r0 in range(0, half_rows, CONV_ROWS)]
                 + [fill_second_half] + [conv_a(r0, half_rows) for r0 in range(0, half_rows, CONV_ROWS)]
                 + [fill_b] + [conv_b(r, r + tm // 4) for r in range(0, tm, tm // 4)])

    for task in mxu_tasks + vpu_tasks:
        task()

    a_tail[...] = a_cur[tm - hl:tm, :]
    a_cur[...] = new["a"]
    c_tail[...] = c_cur[tm - hl:tm, :]
    c_cur[...] = new["cu"]
    gb_cur[...] = new["gb"]


def _in_proj(x2, g, w_all, layer, wa, ba, lg, lb, wb, aw, bsz):
    t, d = x2.shape
    n = w_all.shape[2]
    s = t // bsz
    ka, ca = wa.shape
    kb, cb = wb.shape
    gw = ATT_OUT
    dils = tuple(dil for _, dil in DIL_GROUPS)
    tm = _tile(s, 512)
    nt = s // tm
    n_tiles = t // tm
    hl = CONV_HALO
    sub = V7X_F32_SUBLANES
    assert all(tm % (dil * V7X_BF16_SUBLANES) == 0 for dil in dils if dil > 1)
    assert (tm // 2) % CONV_ROWS == 0 and ka // 2 < hl and kb // 2 < hl and n == 2 * ca + 3 * cb + 3 * aw
    cur = lambda i: (jnp.minimum(i, n_tiles - 1), 0)
    behind = lambda i: (jnp.maximum(i - 1, 0), 0)
    fixed = lambda i: (0, 0)
    qkv_specs, qkv_shapes = [], []
    for dil in dils:
        if dil == 1:
            qkv_specs.append(pl.BlockSpec((tm, 3 * gw), cur))
            qkv_shapes.append(jax.ShapeDtypeStruct((t, 3 * gw), BF16))
        else:
            qkv_specs.append(pl.BlockSpec(
                (1, dil, tm // dil, 3 * gw),
                lambda i: (jnp.minimum(i, n_tiles - 1) // nt, 0, jnp.minimum(i, n_tiles - 1) % nt, 0)))
            qkv_shapes.append(jax.ShapeDtypeStruct((bsz, dil, s // dil, 3 * gw), BF16))
    span = tm // 2 + 2 * hl
    return pl.pallas_call(
        functools.partial(_in_proj_kernel, ca=ca, cb=cb, aw=aw, gw=gw, dils=dils, nt=nt, ka=ka, kb=kb),
        grid=(n_tiles + 1,),
        in_specs=[
            pl.BlockSpec((tm, d), cur),
            pl.BlockSpec((1, d), fixed),
            pl.BlockSpec((None, d, n), lambda i: (layer, 0, 0), pipeline_mode=pl.Buffered(1)),
            pl.BlockSpec((ka, sub, ca), lambda i: (0, 0, 0)), pl.BlockSpec((1, ca), fixed),
            pl.BlockSpec((1, ca), fixed), pl.BlockSpec((1, ca), fixed),
            pl.BlockSpec((kb, cb), fixed),
        ],
        out_specs=[
            pl.BlockSpec((tm, d), cur),
            pl.BlockSpec((tm, ca), behind),
            pl.BlockSpec((tm, cb), behind),
        ] + qkv_specs,
        out_shape=[
            jax.ShapeDtypeStruct((t, d), BF16),
            jax.ShapeDtypeStruct((t, ca), BF16),
            jax.ShapeDtypeStruct((t, cb), BF16),
        ] + qkv_shapes,
        scratch_shapes=[
            pltpu.VMEM((gw // V7X_LANES, tm, V7X_LANES), F32),
            pltpu.VMEM((hl, ca), BF16), pltpu.VMEM((tm, ca), BF16),
            pltpu.VMEM((hl, cb), BF16), pltpu.VMEM((tm, cb), BF16),
            pltpu.VMEM((tm, cb), BF16),
            pltpu.VMEM((sub, span, ca), F32),
            pltpu.VMEM((tm + 2 * hl, cb), F32),
        ],
        compiler_params=_params("arbitrary"),
        name="in_proj_conv",
    )(x2, g, w_all, jnp.broadcast_to(wa[:, None, :], (ka, sub, ca)), ba, lg, lb, wb)


ATT_QROWS = 128


def _attn_kernel(q_ref, kc, kp, kn, vc, vp, vn, o_ref, lse_ref, kbuf, vbuf, *, tq, sub_len, dil, half, slopes):
    i = pl.program_id(2)
    kbuf[0:half, :] = kp[0]
    kbuf[half:half + tq, :] = kc[0]
    kbuf[half + tq:half + tq + half, :] = kn[0]
    vbuf[0:half, :] = vp[0]
    vbuf[half:half + tq, :] = vc[0]
    vbuf[half + tq:half + tq + half, :] = vn[0]

    qr = ATT_QROWS
    kw = qr + 2 * half
    lanes = V7X_LANES
    heads_per_pair = lanes // HEAD_DIM
    rel = (lax.broadcasted_iota(jnp.int32, (qr, kw), 1) - half
           - lax.broadcasted_iota(jnp.int32, (qr, kw), 0))
    arel = jnp.abs(rel)
    band = arel <= half
    dist = (dil * arel).astype(F32)
    biases = [jnp.where(band, -slope * dist, NEG) for slope in slopes]
    lane_head = lax.broadcasted_iota(jnp.int32, (1, lanes), 1) // HEAD_DIM

    for o in range(0, tq, qr):
        kpos = i * tq + (o - half) + lax.broadcasted_iota(jnp.int32, (1, kw), 1)
        pen = jnp.where((kpos >= 0) & (kpos < sub_len), 0.0, NEG)
        for pair in range(q_ref.shape[2] // lanes):
            cols = slice(pair * lanes, (pair + 1) * lanes)
            qp = q_ref[0, o:o + qr, cols]
            kpair = kbuf[o:o + kw, cols]
            vpair = vbuf[o:o + kw, cols]
            out = None
            lse = None
            for hh in range(heads_per_pair):
                sel = lane_head == hh
                qm = jnp.where(sel, qp, jnp.zeros_like(qp))
                s = lax.dot_general(qm, kpair, (((1,), (1,)), ((), ())), preferred_element_type=F32)
                s = s + biases[pair * heads_per_pair + hh] + pen
                m = jnp.max(s, axis=-1, keepdims=True)
                p = jnp.exp(s - m)
                l = jnp.sum(p, axis=-1, keepdims=True)
                pv = _dot(p.astype(BF16), vpair) * (1.0 / l)
                hl = m + jnp.log(l)
                out = pv if out is None else jnp.where(sel, pv, out)
                lse = jnp.broadcast_to(hl, pv.shape) if lse is None else jnp.where(sel, hl, lse)
            o_ref[0, o:o + qr, cols] = out.astype(BF16)
            lse_ref[0, o:o + qr, cols] = lse


def _dilated_attention(qkv, g):
    bsz, dil, sub_len, w3 = qkv.shape
    window, gdil = DIL_GROUPS[g]
    assert gdil == dil
    half = window // (2 * dil)
    gw = w3 // 3
    assert sub_len % half == 0
    tq = _tile(sub_len, 512)
    assert tq % ATT_QROWS == 0 and tq % half == 0 and half % V7X_BF16_SUBLANES == 0
    nh = sub_len // half
    slopes = tuple(float(2.0 ** (-8.0 * (g * HEADS_PER_GROUP + h + 1) / N_HEADS)) for h in range(HEADS_PER_GROUP))

    def spec(rows, which, row_map):
        return pl.BlockSpec((1, None, rows, gw), lambda b, c, i: (b, c, row_map(i), which))

    cur = lambda i: i
    prev = lambda i: jnp.maximum(i * (tq // half) - 1, 0)
    nxt = lambda i: jnp.minimum((i + 1) * (tq // half), nh - 1)
    out_spec = pl.BlockSpec((1, None, tq, gw), lambda b, c, i: (b, c, i, 0))
    return pl.pallas_call(
        functools.partial(_attn_kernel, tq=tq, sub_len=sub_len, dil=dil, half=half, slopes=slopes),
        grid=(bsz, dil, sub_len // tq),
        in_specs=[
            spec(tq, 0, cur),
            spec(tq, 1, cur), spec(half, 1, prev), spec(half, 1, nxt),
            spec(tq, 2, cur), spec(half, 2, prev), spec(half, 2, nxt),
        ],
        out_specs=[out_spec, out_spec],
        out_shape=[jax.ShapeDtypeStruct((bsz, dil, sub_len, gw), BF16),
                   jax.ShapeDtypeStruct((bsz, dil, sub_len, gw), F32)],
        scratch_shapes=[pltpu.VMEM((tq + 2 * half, gw), BF16), pltpu.VMEM((tq + 2 * half, gw), BF16)],
        compiler_params=_params("parallel", "parallel", "parallel"),
        name=f"dilated_attention_g{g}",
    )(qkv, qkv, qkv, qkv, qkv, qkv, qkv)


def _natural_order(ref, nat_ref):
    _, dil, rows, w = ref.shape
    if dil == 1:
        return ref[0, 0].astype(F32)
    lanes = V7X_LANES
    for c in range(dil):
        for s in range(w // lanes):
            nat_ref[s, pl.ds(c, rows, stride=dil), :] = ref[0, c, :, s * lanes:(s + 1) * lanes].astype(F32)
    return jnp.concatenate([nat_ref[s] for s in range(w // lanes)], axis=1)


def _mixer_kernel(x_ref, h_ref, a_ref, b_ref, o0, o1, o2, l0, l1, l2, wa_ref, wb_ref, wc_ref,
                  wga_ref, wgb_ref, wgc_ref, bga_ref, bgb_ref, bgc_ref, wo_ref, out_ref, att_ref, nat_ref):
    j = pl.program_id(1)

    @pl.when(j == 0)
    def _():
        ls = [_natural_order(l, nat_ref) for l in (l0, l1, l2)]
        m = jnp.maximum(jnp.maximum(ls[0], ls[1]), ls[2])
        es = [jnp.exp(l - m) for l in ls]
        inv = 1.0 / (es[0] + es[1] + es[2])
        num = None
        for e, o in zip(es, (o0, o1, o2)):
            term = e * _natural_order(o, nat_ref)
            num = term if num is None else num + term
        att_ref[...] = (num * inv).astype(BF16)
        out_ref[...] = x_ref[...]

    h = h_ref[...]
    mix = (jax.nn.sigmoid(_dot(h, wga_ref[...]) + bga_ref[...]) * _dot(a_ref[...], wa_ref[...])
           + jax.nn.sigmoid(_dot(h, wgb_ref[...]) + bgb_ref[...]) * _dot(b_ref[...], wb_ref[...])
           + jax.nn.sigmoid(_dot(h, wgc_ref[...]) + bgc_ref[...]) * _dot(att_ref[...], wc_ref[...]))
    out_ref[...] += _dot(mix.astype(BF16), wo_ref[...])


def _mixer(x2, h, a, b, os_, ls_, layer, wa, wb, wc, wg, bg, wo):
    t, d = x2.shape
    ca, cb = a.shape[1], b.shape[1]
    bsz, _, _, gw = os_[0].shape
    s = t // bsz
    tm = _tile(s, 512)
    nt = s // tm
    tn = _tile(d, 512)
    nj = d // tn
    row = lambda i, j: (i, 0)
    col = lambda k: (lambda i, j: (layer, 0, k * nj + j))
    att_specs = [pl.BlockSpec((1, o.shape[1], tm // o.shape[1], gw), lambda i, j: (i // nt, 0, i % nt, 0))
                 for o in os_]
    return pl.pallas_call(
        _mixer_kernel,
        grid=(t // tm, nj),
        in_specs=[
            pl.BlockSpec((tm, d), row), pl.BlockSpec((tm, d), row),
            pl.BlockSpec((tm, ca), row), pl.BlockSpec((tm, cb), row),
            *att_specs, *att_specs,
            pl.BlockSpec((None, ca, tn), col(0)), pl.BlockSpec((None, cb, tn), col(0)),
            pl.BlockSpec((None, gw, tn), col(0)),
            pl.BlockSpec((None, d, tn), col(0)), pl.BlockSpec((None, d, tn), col(1)),
            pl.BlockSpec((None, d, tn), col(2)),
            pl.BlockSpec((None, 1, tn), col(0)), pl.BlockSpec((None, 1, tn), col(1)),
            pl.BlockSpec((None, 1, tn), col(2)),
            pl.BlockSpec((None, tn, d), lambda i, j: (layer, j, 0)),
        ],
        out_specs=pl.BlockSpec((tm, d), row),
        out_shape=jax.ShapeDtypeStruct((t, d), F32),
        scratch_shapes=[pltpu.VMEM((tm, gw), BF16), pltpu.VMEM((gw // V7X_LANES, tm, V7X_LANES), F32)],
        compiler_params=_params("parallel", "arbitrary"),
        name="gated_mixer",
    )(x2, h, a, b, *os_, *ls_, wa, wb, wc, wg, wg, wg, bg, bg, bg, wo)


def _ffn_kernel(x_ref, g_ref, wg_ref, wu_ref, wd_ref, *rest, final):
    if final:
        fg_ref, out_ref, h_ref = rest
    else:
        out_ref, h_ref = rest
    j = pl.program_id(1)

    @pl.when(j == 0)
    def _():
        x = x_ref[...]
        h_ref[...] = _rms_normed(x, g_ref[...]).astype(BF16)
        out_ref[...] = x

    h = h_ref[...]
    gate = _dot(h, wg_ref[...])
    f = gate * jax.nn.sigmoid(gate) * _dot(h, wu_ref[...])
    out_ref[...] += _dot(f.astype(BF16), wd_ref[...])

    if final:
        @pl.when(j == pl.num_programs(1) - 1)
        def _():
            out_ref[...] = _rms_normed(out_ref[...], fg_ref[...])


def _ffn(x2, g, layer, wg, wu, wd, final_g=None):
    t, d = x2.shape
    dff = wg.shape[2]
    tm = _tile(t, 1024)
    tf = _tile(dff, 512)
    row = lambda i, j: (i, 0)
    fixed = lambda i, j: (0, 0)
    final = final_g is not None
    in_specs = [
        pl.BlockSpec((tm, d), row), pl.BlockSpec((1, d), fixed),
        pl.BlockSpec((None, d, tf), lambda i, j: (layer, 0, j)),
        pl.BlockSpec((None, d, tf), lambda i, j: (layer, 0, j)),
        pl.BlockSpec((None, tf, d), lambda i, j: (layer, j, 0)),
    ]
    args = [x2, g, wg, wu, wd]
    if final:
        in_specs.append(pl.BlockSpec((1, d), fixed))
        args.append(final_g)
    return pl.pallas_call(
        functools.partial(_ffn_kernel, final=final),
        grid=(t // tm, dff // tf),
        in_specs=in_specs,
        out_specs=pl.BlockSpec((tm, d), row),
        out_shape=jax.ShapeDtypeStruct((t, d), F32),
        scratch_shapes=[pltpu.VMEM((tm, d), BF16)],
        compiler_params=_params("parallel", "arbitrary"),
        name="swiglu_ffn_final" if final else "swiglu_ffn",
    )(*args)


def kernel(x, norm1_g, w_in, b_gate, conv_a_w, conv_a_b, ln_a_g, ln_a_b, w_a_out, conv_b_w, w_b_out,
           w_c_out, w_o, norm2_g, w_ffn_gate, w_ffn_up, w_ffn_down, final_g):
    bsz, s, d = x.shape
    depth = w_in.shape[0]
    ca, cb = conv_a_w.shape[2], conv_b_w.shape[2]
    n_proj = w_in.shape[2] - 3 * d
    aw = (n_proj - 2 * ca - 3 * cb) // 3
    assert aw == N_HEADS * HEAD_DIM and w_c_out.shape[1] == ATT_OUT

    w_proj, w_gate = _cast_bf16(w_in, (n_proj, 3 * d))
    wa_o, wb_o, wc_o, wo = (_cast_bf16(w) for w in (w_a_out, w_b_out, w_c_out, w_o))
    wf_g, wf_u, wf_d = (_cast_bf16(w) for w in (w_ffn_gate, w_ffn_up, w_ffn_down))
    b_gate3 = b_gate.reshape(depth, 1, 3 * d)

    row = lambda v: v.reshape(1, -1)
    x2 = x.reshape(bsz * s, d)
    for l in range(depth):
        h, a_act, b_act, *qkvs = _in_proj(
            x2, row(norm1_g[l]), w_proj, l, conv_a_w[l], row(conv_a_b[l]), row(ln_a_g[l]), row(ln_a_b[l]),
            conv_b_w[l], aw, bsz)
        att = [_dilated_attention(qkv.reshape(bsz, dil, s // dil, 3 * ATT_OUT), g)
               for g, (qkv, (_, dil)) in enumerate(zip(qkvs, DIL_GROUPS))]
        x2 = _mixer(x2, h, a_act, b_act, [o for o, _ in att], [ls for _, ls in att],
                    l, wa_o, wb_o, wc_o, w_gate, b_gate3, wo)
        x2 = _ffn(x2, row(norm2_g[l]), l, wf_g, wf_u, wf_d, row(final_g) if l == depth - 1 else None)
    return x2.reshape(bsz, s, d)
```

```python
import functools

import jax
import jax.numpy as jnp
from jax import lax
from jax.experimental import pallas as pl
from jax.experimental.pallas import tpu as pltpu

F32 = jnp.float32
BF16 = jnp.bfloat16

HEAD_DIM = 64
N_HEADS = 12
DIL_GROUPS = ((128, 1), (512, 4), (2048, 16))
HEADS_PER_GROUP = N_HEADS // len(DIL_GROUPS)
ATT_OUT = HEADS_PER_GROUP * HEAD_DIM
NORM_EPS = 1e-6
LN_EPS = 1e-5
NEG = -1e30

V7X_LANES = 128
V7X_F32_SUBLANES = 8
V7X_BF16_SUBLANES = 16
V7X_VMEM_BYTES = 64 * 1024 * 1024
VMEM_LIMIT_BYTES = V7X_VMEM_BYTES - 8 * 1024 * 1024

MIXER_COLS = 512
FFN_COLS = 512


def _params(*semantics):
    return pltpu.CompilerParams(dimension_semantics=semantics, vmem_limit_bytes=VMEM_LIMIT_BYTES)


def _tile(n, target):
    t = min(n, target)
    assert n % t == 0, (n, t)
    return t


def _rms_normed(x, g):
    ms = jnp.mean(x * x, axis=-1, keepdims=True)
    return x * lax.rsqrt(ms + NORM_EPS) * g


def _dot(a, b):
    return jnp.dot(a, b, preferred_element_type=F32)


CAST_BLOCK_BYTES = 4 * 1024 * 1024


def _repack_kernel(*refs, n_in, plans):
    ins, outs = refs[:n_in], refs[n_in:]
    for o_ref, plan in zip(outs, plans):
        if plan[0] == "plain":
            _, k, c0, width = plan
            o_ref[0] = ins[k][0, :, c0:c0 + width].astype(BF16)
        else:
            for j, pieces in enumerate(plan[1]):
                off = 0
                for k, c0, width in pieces:
                    o_ref[0, j, :, off:off + width] = ins[k][0, :, c0:c0 + width].astype(BF16)
                    off += width


def _repack_bf16(ws, plans):
    nl, r, _ = ws[0].shape
    assert all(w.shape[:2] == (nl, r) for w in ws)
    row_bytes = 4 * sum(w.shape[2] for w in ws)
    rb = r
    while rb * row_bytes > CAST_BLOCK_BYTES and rb % (2 * V7X_BF16_SUBLANES) == 0:
        rb //= 2
    out_specs, out_shapes = [], []
    for plan in plans:
        if plan[0] == "plain":
            width = plan[3]
            out_specs.append(pl.BlockSpec((1, rb, width), lambda l, i: (l, i, 0)))
            out_shapes.append(jax.ShapeDtypeStruct((nl, r, width), BF16))
        else:
            nj = len(plan[1])
            width = sum(p[2] for p in plan[1][0])
            assert all(sum(p[2] for p in pieces) == width for pieces in plan[1])
            out_specs.append(pl.BlockSpec((1, nj, rb, width), lambda l, i: (l, 0, i, 0)))
            out_shapes.append(jax.ShapeDtypeStruct((nl, nj, r, width), BF16))
    outs = pl.pallas_call(
        functools.partial(_repack_kernel, n_in=len(ws), plans=plans),
        grid=(nl, r // rb),
        in_specs=[pl.BlockSpec((1, rb, w.shape[2]), lambda l, i: (l, i, 0)) for w in ws],
        out_specs=out_specs,
        out_shape=out_shapes,
        compiler_params=_params("parallel", "parallel"),
        name="repack_bf16",
    )(*ws)
    return outs


def _cast_bf16(w):
    return _repack_bf16([w], [("plain", 0, 0, w.shape[2])])[0]


def _stack_rows_kernel(*refs, tn):
    ins, o_ref = refs[:-1], refs[-1]
    for j in range(o_ref.shape[1]):
        off = 0
        for w_ref in ins:
            rows = w_ref.shape[1]
            o_ref[0, j, off:off + rows, :] = w_ref[0, :, j * tn:(j + 1) * tn].astype(BF16)
            off += rows


def _stack_rows_bf16(ws, tn):
    nl, _, c = ws[0].shape
    rows = sum(w.shape[1] for w in ws)
    return pl.pallas_call(
        functools.partial(_stack_rows_kernel, tn=tn),
        grid=(nl,),
        in_specs=[pl.BlockSpec((1, w.shape[1], c), lambda l: (l, 0, 0)) for w in ws],
        out_specs=pl.BlockSpec((1, c // tn, rows, tn), lambda l: (l, 0, 0, 0)),
        out_shape=jax.ShapeDtypeStruct((nl, c // tn, rows, tn), BF16),
        compiler_params=_params("parallel"),
        name="stack_rows_bf16",
    )(*ws)


CONV_HALO = V7X_BF16_SUBLANES
CONV_ROWS = 32


def _shifted_copies(abuf):
    sub = V7X_F32_SUBLANES
    n_sh = abuf.shape[1] - sub
    for s in range(1, sub):
        abuf[s, 0:n_sh, :] = abuf[0, s:s + n_sh, :]


def _conv_a_chunk(abuf, wa_ref, ba_ref, lg_ref, lb_ref, out_ref, r0, out_row0, ka):
    sub = V7X_F32_SUBLANES
    acc = jnp.zeros((CONV_ROWS // sub, sub, abuf.shape[2]), F32)
    for j in range(ka):
        s = CONV_HALO + r0 + j - ka // 2
        al = s // sub * sub
        acc = acc + wa_ref[j] * abuf[s % sub, al:al + CONV_ROWS, :].reshape(acc.shape)
    acc = acc.reshape(CONV_ROWS, abuf.shape[2]) + ba_ref[...]
    mu = jnp.mean(acc, axis=-1, keepdims=True)
    cen = acc - mu
    var = jnp.mean(cen * cen, axis=-1, keepdims=True)
    y = cen * lax.rsqrt(var + LN_EPS) * lg_ref[...] + lb_ref[...]
    out_ref[out_row0 + r0:out_row0 + r0 + CONV_ROWS, :] = (y * jax.nn.sigmoid(y)).astype(BF16)


def _in_proj_kernel(x_ref, g_ref, w_ref, wa_ref, ba_ref, lg_ref, lb_ref, wb_ref,
                    h_ref, aact_ref, bact_ref, *rest, ca, cb, aw, gw, dils, nt, ka, kb):
    ng = len(dils)
    qkv_refs = rest[:ng]
    zs, a_tail, a_cur, c_tail, c_cur, gb_cur, abuf, cbuf = rest[ng:]
    i = pl.program_id(0)
    tm = x_ref.shape[0]
    hl = CONV_HALO
    lanes = V7X_LANES

    @pl.when(i == 0)
    def _():
        for ref in (a_tail, a_cur, c_tail, c_cur, gb_cur):
            ref[...] = jnp.zeros(ref.shape, ref.dtype)

    h = _rms_normed(x_ref[...], g_ref[...]).astype(BF16)
    h_ref[...] = h

    def proj(c0, c1):
        return _dot(h, w_ref[:, c0:c1])

    a_new = (proj(0, ca) * jax.nn.sigmoid(proj(ca, 2 * ca))).astype(BF16)
    o = 2 * ca
    gb_new = proj(o, o + cb).astype(BF16)
    cu_new = (proj(o + cb, o + 2 * cb) * proj(o + 2 * cb, o + 3 * cb)).astype(BF16)
    o += 3 * cb
    for g, dil in enumerate(dils):
        ref = qkv_refs[g]
        for n in range(3):
            z = proj(o + n * aw + g * gw, o + n * aw + (g + 1) * gw)
            if n == 0:
                z = z * (HEAD_DIM ** -0.5)
            if dil == 1:
                ref[:, n * gw:(n + 1) * gw] = z.astype(BF16)
                continue
            for s in range(gw // lanes):
                zs[s] = z[:, s * lanes:(s + 1) * lanes]
            for c in range(dil):
                for s in range(gw // lanes):
                    ref[0, c, :, n * gw + s * lanes:n * gw + (s + 1) * lanes] = (
                        zs[s, pl.ds(c, tm // dil, stride=dil), :].astype(BF16))

    p = lax.rem(i + nt - 1, nt)
    first, last = p == 0, p == nt - 1
    half_rows = tm // 2
    span = half_rows + 2 * hl
    abuf[0, 0:hl, :] = jnp.where(first, 0.0, a_tail[...].astype(F32))
    abuf[0, hl:span, :] = a_cur[0:span - hl, :].astype(F32)
    _shifted_copies(abuf)
    for r0 in range(0, half_rows, CONV_ROWS):
        _conv_a_chunk(abuf, wa_ref, ba_ref, lg_ref, lb_ref, aact_ref, r0, 0, ka)
    abuf[0, 0:span - hl, :] = a_cur[half_rows - hl:tm, :].astype(F32)
    abuf[0, span - hl:span, :] = jnp.where(last, 0.0, a_new[0:hl, :].astype(F32))
    _shifted_copies(abuf)
    for r0 in range(0, half_rows, CONV_ROWS):
        _conv_a_chunk(abuf, wa_ref, ba_ref, lg_ref, lb_ref, aact_ref, r0, half_rows, ka)

    cbuf[0:hl, :] = jnp.where(first, 0.0, c_tail[...].astype(F32))
    cbuf[hl:hl + tm, :] = c_cur[...].astype(F32)
    cbuf[hl + tm:hl + tm + hl, :] = jnp.where(last, 0.0, cu_new[0:hl, :].astype(F32))
    for r0 in range(0, tm, CONV_ROWS):
        accb = jnp.zeros((CONV_ROWS, cb), F32)
        for j in range(kb):
            s = hl + r0 + j - kb // 2
            accb = accb + wb_ref[j:j + 1, :] * cbuf[s:s + CONV_ROWS, :]
        bact_ref[r0:r0 + CONV_ROWS, :] = (gb_cur[r0:r0 + CONV_ROWS, :].astype(F32) * accb).astype(BF16)

    a_tail[...] = a_cur[tm - hl:tm, :]
    a_cur[...] = a_new
    c_tail[...] = c_cur[tm - hl:tm, :]
    c_cur[...] = cu_new
    gb_cur[...] = gb_new


def _in_proj(x2, g, w_all, layer, wa, ba, lg, lb, wb, aw, bsz):
    t, d = x2.shape
    n = w_all.shape[2]
    s = t // bsz
    ka, ca = wa.shape
    kb, cb = wb.shape
    gw = ATT_OUT
    dils = tuple(dil for _, dil in DIL_GROUPS)
    tm = _tile(s, 512)
    nt = s // tm
    n_tiles = t // tm
    hl = CONV_HALO
    sub = V7X_F32_SUBLANES
    assert all(tm % (dil * V7X_BF16_SUBLANES) == 0 for dil in dils if dil > 1)
    assert (tm // 2) % CONV_ROWS == 0 and ka // 2 < hl and kb // 2 < hl and n == 2 * ca + 3 * cb + 3 * aw
    cur = lambda i: (jnp.minimum(i, n_tiles - 1), 0)
    behind = lambda i: (jnp.maximum(i - 1, 0), 0)
    fixed = lambda i: (0, 0)
    qkv_specs, qkv_shapes = [], []
    for dil in dils:
        if dil == 1:
            qkv_specs.append(pl.BlockSpec((tm, 3 * gw), cur))
            qkv_shapes.append(jax.ShapeDtypeStruct((t, 3 * gw), BF16))
        else:
            qkv_specs.append(pl.BlockSpec(
                (1, dil, tm // dil, 3 * gw),
                lambda i: (jnp.minimum(i, n_tiles - 1) // nt, 0, jnp.minimum(i, n_tiles - 1) % nt, 0)))
            qkv_shapes.append(jax.ShapeDtypeStruct((bsz, dil, s // dil, 3 * gw), BF16))
    span = tm // 2 + 2 * hl
    return pl.pallas_call(
        functools.partial(_in_proj_kernel, ca=ca, cb=cb, aw=aw, gw=gw, dils=dils, nt=nt, ka=ka, kb=kb),
        grid=(n_tiles + 1,),
        in_specs=[
            pl.BlockSpec((tm, d), cur),
            pl.BlockSpec((1, d), fixed),
            pl.BlockSpec((None, d, n), lambda i: (layer, 0, 0), pipeline_mode=pl.Buffered(1)),
            pl.BlockSpec((ka, sub, ca), lambda i: (0, 0, 0)), pl.BlockSpec((1, ca), fixed),
            pl.BlockSpec((1, ca), fixed), pl.BlockSpec((1, ca), fixed),
            pl.BlockSpec((kb, cb), fixed),
        ],
        out_specs=[
            pl.BlockSpec((tm, d), cur),
            pl.BlockSpec((tm, ca), behind),
            pl.BlockSpec((tm, cb), behind),
        ] + qkv_specs,
        out_shape=[
            jax.ShapeDtypeStruct((t, d), BF16),
            jax.ShapeDtypeStruct((t, ca), BF16),
            jax.ShapeDtypeStruct((t, cb), BF16),
        ] + qkv_shapes,
        scratch_shapes=[
            pltpu.VMEM((gw // V7X_LANES, tm, V7X_LANES), F32),
            pltpu.VMEM((hl, ca), BF16), pltpu.VMEM((tm, ca), BF16),
            pltpu.VMEM((hl, cb), BF16), pltpu.VMEM((tm, cb), BF16),
            pltpu.VMEM((tm, cb), BF16),
            pltpu.VMEM((sub, span, ca), F32),
            pltpu.VMEM((tm + 2 * hl, cb), F32),
        ],
        compiler_params=_params("arbitrary"),
        name="in_proj_conv",
    )(x2, g, w_all, jnp.broadcast_to(wa[:, None, :], (ka, sub, ca)), ba, lg, lb, wb)


ATT_QROWS = 128


def _attn_kernel(q_ref, kc, kp, kn, vc, vp, vn, o_ref, lse_ref, kbuf, vbuf, *, tq, sub_len, dil, half, slopes):
    i = pl.program_id(2)
    kbuf[0:half, :] = kp[0]
    kbuf[half:half + tq, :] = kc[0]
    kbuf[half + tq:half + tq + half, :] = kn[0]
    vbuf[0:half, :] = vp[0]
    vbuf[half:half + tq, :] = vc[0]
    vbuf[half + tq:half + tq + half, :] = vn[0]

    qr = ATT_QROWS
    kw = qr + 2 * half
    lanes = V7X_LANES
    heads_per_pair = lanes // HEAD_DIM
    rel = (lax.broadcasted_iota(jnp.int32, (qr, kw), 1) - half
           - lax.broadcasted_iota(jnp.int32, (qr, kw), 0))
    arel = jnp.abs(rel)
    band = arel <= half
    dist = (dil * arel).astype(F32)
    biases = [jnp.where(band, -slope * dist, NEG) for slope in slopes]
    lane_head = lax.broadcasted_iota(jnp.int32, (1, lanes), 1) // HEAD_DIM

    for o in range(0, tq, qr):
        kpos = i * tq + (o - half) + lax.broadcasted_iota(jnp.int32, (1, kw), 1)
        pen = jnp.where((kpos >= 0) & (kpos < sub_len), 0.0, NEG)
        for pair in range(q_ref.shape[2] // lanes):
            cols = slice(pair * lanes, (pair + 1) * lanes)
            qp = q_ref[0, o:o + qr, cols]
            kpair = kbuf[o:o + kw, cols]
            vpair = vbuf[o:o + kw, cols]
            out = None
            lse = None
            for hh in range(heads_per_pair):
                sel = lane_head == hh
                qm = jnp.where(sel, qp, jnp.zeros_like(qp))
                s = lax.dot_general(qm, kpair, (((1,), (1,)), ((), ())), preferred_element_type=F32)
                s = s + biases[pair * heads_per_pair + hh] + pen
                m = jnp.max(s, axis=-1, keepdims=True)
                p = jnp.exp(s - m)
                l = jnp.sum(p, axis=-1, keepdims=True)
                pv = _dot(p.astype(BF16), vpair) * (1.0 / l)
                hl = m + jnp.log(l)
                out = pv if out is None else jnp.where(sel, pv, out)
                lse = jnp.broadcast_to(hl, pv.shape) if lse is None else jnp.where(sel, hl, lse)
            o_ref[0, o:o + qr, cols] = out.astype(BF16)
            lse_ref[0, o:o + qr, cols] = lse


def _dilated_attention(qkv, g):
    bsz, dil, sub_len, w3 = qkv.shape
    window, gdil = DIL_GROUPS[g]
    assert gdil == dil
    half = window // (2 * dil)
    gw = w3 // 3
    assert sub_len % half == 0
    tq = _tile(sub_len, 512)
    assert tq % ATT_QROWS == 0 and tq % half == 0 and half % V7X_BF16_SUBLANES == 0
    nh = sub_len // half
    slopes = tuple(float(2.0 ** (-8.0 * (g * HEADS_PER_GROUP + h + 1) / N_HEADS)) for h in range(HEADS_PER_GROUP))

    def spec(rows, which, row_map):
        return pl.BlockSpec((1, None, rows, gw), lambda b, c, i: (b, c, row_map(i), which))

    cur = lambda i: i
    prev = lambda i: jnp.maximum(i * (tq // half) - 1, 0)
    nxt = lambda i: jnp.minimum((i + 1) * (tq // half), nh - 1)
    out_spec = pl.BlockSpec((1, None, tq, gw), lambda b, c, i: (b, c, i, 0))
    return pl.pallas_call(
        functools.partial(_attn_kernel, tq=tq, sub_len=sub_len, dil=dil, half=half, slopes=slopes),
        grid=(bsz, dil, sub_len // tq),
        in_specs=[
            spec(tq, 0, cur),
            spec(tq, 1, cur), spec(half, 1, prev), spec(half, 1, nxt),
            spec(tq, 2, cur), spec(half, 2, prev), spec(half, 2, nxt),
        ],
        out_specs=[out_spec, out_spec],
        out_shape=[jax.ShapeDtypeStruct((bsz, dil, sub_len, gw), BF16),
                   jax.ShapeDtypeStruct((bsz, dil, sub_len, gw), F32)],
        scratch_shapes=[pltpu.VMEM((tq + 2 * half, gw), BF16), pltpu.VMEM((tq + 2 * half, gw), BF16)],
        compiler_params=_params("parallel", "parallel", "parallel"),
        name=f"dilated_attention_g{g}",
    )(qkv, qkv, qkv, qkv, qkv, qkv, qkv)


def _natural_order(ref, nat_ref):
    _, dil, rows, w = ref.shape
    if dil == 1:
        return ref[0, 0].astype(F32)
    lanes = V7X_LANES
    for c in range(dil):
        for s in range(w // lanes):
            nat_ref[s, pl.ds(c, rows, stride=dil), :] = ref[0, c, :, s * lanes:(s + 1) * lanes].astype(F32)
    return jnp.concatenate([nat_ref[s] for s in range(w // lanes)], axis=1)


def _mixer_kernel(x_ref, h_ref, a_ref, b_ref, o0, o1, o2, l0, l1, l2, wg_ref, wbr_ref, bg_ref, wo_ref,
                  out_ref, att_ref, nat_ref):
    j = pl.program_id(1)

    @pl.when(j == 0)
    def _():
        ls = [_natural_order(l, nat_ref) for l in (l0, l1, l2)]
        m = jnp.maximum(jnp.maximum(ls[0], ls[1]), ls[2])
        es = [jnp.exp(l - m) for l in ls]
        inv = 1.0 / (es[0] + es[1] + es[2])
        num = None
        for e, o in zip(es, (o0, o1, o2)):
            term = e * _natural_order(o, nat_ref)
            num = term if num is None else num + term
        att_ref[...] = (num * inv).astype(BF16)
        out_ref[...] = x_ref[...]

    tn = wo_ref.shape[0]
    ca, cb = a_ref.shape[1], b_ref.shape[1]
    gates = jax.nn.sigmoid(_dot(h_ref[...], wg_ref[...]) + bg_ref[...])
    mix = (gates[:, 0:tn] * _dot(a_ref[...], wbr_ref[0:ca, :])
           + gates[:, tn:2 * tn] * _dot(b_ref[...], wbr_ref[ca:ca + cb, :])
           + gates[:, 2 * tn:3 * tn] * _dot(att_ref[...], wbr_ref[ca + cb:, :]))
    out_ref[...] += _dot(mix.astype(BF16), wo_ref[...])


def _mixer(x2, h, a, b, os_, ls_, layer, w_gate, w_branch, b_gate, wo):
    t, d = x2.shape
    ca, cb = a.shape[1], b.shape[1]
    bsz, _, _, gw = os_[0].shape
    s = t // bsz
    tm = _tile(s, 512)
    nt = s // tm
    nj, tn = w_branch.shape[1], w_branch.shape[3]
    assert nj * tn == d and w_gate.shape[1:] == (nj, d, 3 * tn) and w_branch.shape[2] == ca + cb + gw
    row = lambda i, j: (i, 0)
    chunk = lambda i, j: (layer, j, 0, 0)
    att_specs = [pl.BlockSpec((1, o.shape[1], tm // o.shape[1], gw), lambda i, j: (i // nt, 0, i % nt, 0))
                 for o in os_]
    return pl.pallas_call(
        _mixer_kernel,
        grid=(t // tm, nj),
        in_specs=[
            pl.BlockSpec((tm, d), row), pl.BlockSpec((tm, d), row),
            pl.BlockSpec((tm, ca), row), pl.BlockSpec((tm, cb), row),
            *att_specs, *att_specs,
            pl.BlockSpec((None, None, d, 3 * tn), chunk),
            pl.BlockSpec((None, None, ca + cb + gw, tn), chunk),
            pl.BlockSpec((None, None, 1, 3 * tn), chunk),
            pl.BlockSpec((None, tn, d), lambda i, j: (layer, j, 0)),
        ],
        out_specs=pl.BlockSpec((tm, d), row),
        out_shape=jax.ShapeDtypeStruct((t, d), F32),
        scratch_shapes=[pltpu.VMEM((tm, gw), BF16), pltpu.VMEM((gw // V7X_LANES, tm, V7X_LANES), F32)],
        compiler_params=_params("parallel", "arbitrary"),
        name="gated_mixer",
    )(x2, h, a, b, *os_, *ls_, w_gate, w_branch, b_gate, wo)


def _ffn_kernel(x_ref, g_ref, wgu_ref, wd_ref, *rest, final):
    if final:
        fg_ref, out_ref, h_ref = rest
    else:
        out_ref, h_ref = rest
    j = pl.program_id(1)

    @pl.when(j == 0)
    def _():
        x = x_ref[...]
        h_ref[...] = _rms_normed(x, g_ref[...]).astype(BF16)
        out_ref[...] = x

    tf = wd_ref.shape[0]
    gu = _dot(h_ref[...], wgu_ref[...])
    gate = gu[:, 0:tf]
    f = gate * jax.nn.sigmoid(gate) * gu[:, tf:2 * tf]
    out_ref[...] += _dot(f.astype(BF16), wd_ref[...])

    if final:
        @pl.when(j == pl.num_programs(1) - 1)
        def _():
            out_ref[...] = _rms_normed(out_ref[...], fg_ref[...])


def _ffn(x2, g, layer, w_gate_up, wd, final_g=None):
    t, d = x2.shape
    nf, tf = w_gate_up.shape[1], w_gate_up.shape[3] // 2
    assert nf * tf == wd.shape[1]
    tm = _tile(t, 1024)
    row = lambda i, j: (i, 0)
    fixed = lambda i, j: (0, 0)
    final = final_g is not None
    in_specs = [
        pl.BlockSpec((tm, d), row), pl.BlockSpec((1, d), fixed),
        pl.BlockSpec((None, None, d, 2 * tf), lambda i, j: (layer, j, 0, 0)),
        pl.BlockSpec((None, tf, d), lambda i, j: (layer, j, 0)),
    ]
    args = [x2, g, w_gate_up, wd]
    if final:
        in_specs.append(pl.BlockSpec((1, d), fixed))
        args.append(final_g)
    return pl.pallas_call(
        functools.partial(_ffn_kernel, final=final),
        grid=(t // tm, nf),
        in_specs=in_specs,
        out_specs=pl.BlockSpec((tm, d), row),
        out_shape=jax.ShapeDtypeStruct((t, d), F32),
        scratch_shapes=[pltpu.VMEM((tm, d), BF16)],
        compiler_params=_params("parallel", "arbitrary"),
        name="swiglu_ffn_final" if final else "swiglu_ffn",
    )(*args)


def kernel(x, norm1_g, w_in, b_gate, conv_a_w, conv_a_b, ln_a_g, ln_a_b, w_a_out, conv_b_w, w_b_out,
           w_c_out, w_o, norm2_g, w_ffn_gate, w_ffn_up, w_ffn_down, final_g):
    bsz, s, d = x.shape
    depth = w_in.shape[0]
    ca, cb = conv_a_w.shape[2], conv_b_w.shape[2]
    n_proj = w_in.shape[2] - 3 * d
    aw = (n_proj - 2 * ca - 3 * cb) // 3
    dff = w_ffn_gate.shape[2]
    assert aw == N_HEADS * HEAD_DIM and w_c_out.shape[1] == ATT_OUT

    tn = _tile(d, MIXER_COLS)
    nj = d // tn
    tf = _tile(dff, FFN_COLS)
    gate_chunks = [[(0, n_proj + k * d + j * tn, tn) for k in range(3)] for j in range(nj)]
    w_proj, w_gate = _repack_bf16([w_in], [("plain", 0, 0, n_proj), ("chunked", gate_chunks)])
    b_gate_c = b_gate.reshape(depth, 3, nj, tn).transpose(0, 2, 1, 3).reshape(depth, nj, 1, 3 * tn)
    w_branch = _stack_rows_bf16([w_a_out, w_b_out, w_c_out], tn)
    wo = _cast_bf16(w_o)
    w_gate_up = _repack_bf16(
        [w_ffn_gate, w_ffn_up], [("chunked", [[(0, j * tf, tf), (1, j * tf, tf)] for j in range(dff // tf)])])[0]
    w_down = _cast_bf16(w_ffn_down)

    row = lambda v: v.reshape(1, -1)
    x2 = x.reshape(bsz * s, d)
    for l in range(depth):
        h, a_act, b_act, *qkvs = _in_proj(
            x2, row(norm1_g[l]), w_proj, l, conv_a_w[l], row(conv_a_b[l]), row(ln_a_g[l]), row(ln_a_b[l]),
            conv_b_w[l], aw, bsz)
        att = [_dilated_attention(qkv.reshape(bsz, dil, s // dil, 3 * ATT_OUT), g)
               for g, (qkv, (_, dil)) in enumerate(zip(qkvs, DIL_GROUPS))]
        x2 = _mixer(x2, h, a_act, b_act, [o for o, _ in att], [ls for _, ls in att],
                    l, w_gate, w_branch, b_gate_c, wo)
        x2 = _ffn(x2, row(norm2_g[l]), l, w_gate_up, w_down, row(final_g) if l == depth - 1 else None)
    return x2.reshape(bsz, s, d)
```

```python
import functools

import jax
import jax.numpy as jnp
from jax import lax
from jax.experimental import pallas as pl
from jax.experimental.pallas import tpu as pltpu

F32 = jnp.float32
BF16 = jnp.bfloat16

HEAD_DIM = 64
N_HEADS = 12
DIL_GROUPS = ((128, 1), (512, 4), (2048, 16))
HEADS_PER_GROUP = N_HEADS // len(DIL_GROUPS)
ATT_OUT = HEADS_PER_GROUP * HEAD_DIM
NORM_EPS = 1e-6
LN_EPS = 1e-5
NEG = -1e30

V7X_LANES = 128
V7X_F32_SUBLANES = 8
V7X_BF16_SUBLANES = 16
V7X_VMEM_BYTES = 64 * 1024 * 1024
VMEM_LIMIT_BYTES = V7X_VMEM_BYTES - 8 * 1024 * 1024

MIXER_COLS = 512
FFN_COLS = 512


def _params(*semantics):
    return pltpu.CompilerParams(dimension_semantics=semantics, vmem_limit_bytes=VMEM_LIMIT_BYTES)


def _tile(n, target):
    t = min(n, target)
    assert n % t == 0, (n, t)
    return t


def _rms_normed(x, g):
    ms = jnp.mean(x * x, axis=-1, keepdims=True)
    return x * lax.rsqrt(ms + NORM_EPS) * g


def _dot(a, b):
    return jnp.dot(a, b, preferred_element_type=F32)


CAST_BLOCK_BYTES = 4 * 1024 * 1024


def _repack_kernel(*refs, n_in, plans):
    ins, outs = refs[:n_in], refs[n_in:]
    for o_ref, plan in zip(outs, plans):
        if plan[0] == "plain":
            _, k, c0, width = plan
            o_ref[0] = ins[k][0, :, c0:c0 + width].astype(BF16)
        else:
            for j, pieces in enumerate(plan[1]):
                off = 0
                for k, c0, width in pieces:
                    o_ref[0, j, :, off:off + width] = ins[k][0, :, c0:c0 + width].astype(BF16)
                    off += width


def _repack_bf16(ws, plans):
    nl, r, _ = ws[0].shape
    assert all(w.shape[:2] == (nl, r) for w in ws)
    row_bytes = 4 * sum(w.shape[2] for w in ws)
    rb = r
    while rb * row_bytes > CAST_BLOCK_BYTES and rb % (2 * V7X_BF16_SUBLANES) == 0:
        rb //= 2
    out_specs, out_shapes = [], []
    for plan in plans:
        if plan[0] == "plain":
            width = plan[3]
            out_specs.append(pl.BlockSpec((1, rb, width), lambda l, i: (l, i, 0)))
            out_shapes.append(jax.ShapeDtypeStruct((nl, r, width), BF16))
        else:
            nj = len(plan[1])
            width = sum(p[2] for p in plan[1][0])
            assert all(sum(p[2] for p in pieces) == width for pieces in plan[1])
            out_specs.append(pl.BlockSpec((1, nj, rb, width), lambda l, i: (l, 0, i, 0)))
            out_shapes.append(jax.ShapeDtypeStruct((nl, nj, r, width), BF16))
    outs = pl.pallas_call(
        functools.partial(_repack_kernel, n_in=len(ws), plans=plans),
        grid=(nl, r // rb),
        in_specs=[pl.BlockSpec((1, rb, w.shape[2]), lambda l, i: (l, i, 0)) for w in ws],
        out_specs=out_specs,
        out_shape=out_shapes,
        compiler_params=_params("parallel", "parallel"),
        name="repack_bf16",
    )(*ws)
    return outs


def _cast_bf16(w):
    return _repack_bf16([w], [("plain", 0, 0, w.shape[2])])[0]


def _stack_rows_kernel(*refs, tn):
    ins, o_ref = refs[:-1], refs[-1]
    for j in range(o_ref.shape[1]):
        off = 0
        for w_ref in ins:
            rows = w_ref.shape[1]
            o_ref[0, j, off:off + rows, :] = w_ref[0, :, j * tn:(j + 1) * tn].astype(BF16)
            off += rows


def _stack_rows_bf16(ws, tn):
    nl, _, c = ws[0].shape
    rows = sum(w.shape[1] for w in ws)
    return pl.pallas_call(
        functools.partial(_stack_rows_kernel, tn=tn),
        grid=(nl,),
        in_specs=[pl.BlockSpec((1, w.shape[1], c), lambda l: (l, 0, 0)) for w in ws],
        out_specs=pl.BlockSpec((1, c // tn, rows, tn), lambda l: (l, 0, 0, 0)),
        out_shape=jax.ShapeDtypeStruct((nl, c // tn, rows, tn), BF16),
        compiler_params=_params("parallel"),
        name="stack_rows_bf16",
    )(*ws)


CONV_HALO = V7X_BF16_SUBLANES
CONV_ROWS = 32


def _shifted_copies(abuf):
    sub = V7X_F32_SUBLANES
    n_sh = abuf.shape[1] - sub
    for s in range(1, sub):
        abuf[s, 0:n_sh, :] = abuf[0, s:s + n_sh, :]


def _conv_a_chunk(abuf, wa_ref, ba_ref, lg_ref, lb_ref, out_ref, r0, out_row0, ka):
    sub = V7X_F32_SUBLANES
    acc = jnp.zeros((CONV_ROWS // sub, sub, abuf.shape[2]), F32)
    for j in range(ka):
        s = CONV_HALO + r0 + j - ka // 2
        al = s // sub * sub
        acc = acc + wa_ref[j] * abuf[s % sub, al:al + CONV_ROWS, :].reshape(acc.shape)
    acc = acc.reshape(CONV_ROWS, abuf.shape[2]) + ba_ref[...]
    mu = jnp.mean(acc, axis=-1, keepdims=True)
    cen = acc - mu
    var = jnp.mean(cen * cen, axis=-1, keepdims=True)
    y = cen * lax.rsqrt(var + LN_EPS) * lg_ref[...] + lb_ref[...]
    out_ref[out_row0 + r0:out_row0 + r0 + CONV_ROWS, :] = (y * jax.nn.sigmoid(y)).astype(BF16)


def _in_proj_kernel(x_ref, g_ref, w_ref, wa_ref, ba_ref, lg_ref, lb_ref, wb_ref,
                    h_ref, aact_ref, bact_ref, *rest, ca, cb, aw, gw, dils, nt, ka, kb):
    ng = len(dils)
    qkv_refs = rest[:ng]
    zs, a_tail, a_cur, c_tail, c_cur, gb_cur, abuf, cbuf = rest[ng:]
    i = pl.program_id(0)
    tm = x_ref.shape[0]
    hl = CONV_HALO
    lanes = V7X_LANES

    @pl.when(i == 0)
    def _():
        for ref in (a_tail, a_cur, c_tail, c_cur, gb_cur):
            ref[...] = jnp.zeros(ref.shape, ref.dtype)

    h = _rms_normed(x_ref[...], g_ref[...]).astype(BF16)
    h_ref[...] = h

    def proj(c0, c1):
        return _dot(h, w_ref[:, c0:c1])

    a_new = (proj(0, ca) * jax.nn.sigmoid(proj(ca, 2 * ca))).astype(BF16)
    o = 2 * ca
    gb_new = proj(o, o + cb).astype(BF16)
    cu_new = (proj(o + cb, o + 2 * cb) * proj(o + 2 * cb, o + 3 * cb)).astype(BF16)
    o += 3 * cb
    for g, dil in enumerate(dils):
        ref = qkv_refs[g]
        for n in range(3):
            z = proj(o + n * aw + g * gw, o + n * aw + (g + 1) * gw)
            if n == 0:
                z = z * (HEAD_DIM ** -0.5)
            if dil == 1:
                ref[:, n * gw:(n + 1) * gw] = z.astype(BF16)
                continue
            for s in range(gw // lanes):
                zs[s] = z[:, s * lanes:(s + 1) * lanes]
            for c in range(dil):
                for s in range(gw // lanes):
                    ref[0, c, :, n * gw + s * lanes:n * gw + (s + 1) * lanes] = (
                        zs[s, pl.ds(c, tm // dil, stride=dil), :].astype(BF16))

    p = lax.rem(i + nt - 1, nt)
    first, last = p == 0, p == nt - 1
    half_rows = tm // 2
    span = half_rows + 2 * hl
    abuf[0, 0:hl, :] = jnp.where(first, 0.0, a_tail[...].astype(F32))
    abuf[0, hl:span, :] = a_cur[0:span - hl, :].astype(F32)
    _shifted_copies(abuf)
    for r0 in range(0, half_rows, CONV_ROWS):
        _conv_a_chunk(abuf, wa_ref, ba_ref, lg_ref, lb_ref, aact_ref, r0, 0, ka)
    abuf[0, 0:span - hl, :] = a_cur[half_rows - hl:tm, :].astype(F32)
    abuf[0, span - hl:span, :] = jnp.where(last, 0.0, a_new[0:hl, :].astype(F32))
    _shifted_copies(abuf)
    for r0 in range(0, half_rows, CONV_ROWS):
        _conv_a_chunk(abuf, wa_ref, ba_ref, lg_ref, lb_ref, aact_ref, r0, half_rows, ka)

    cbuf[0:hl, :] = jnp.where(first, 0.0, c_tail[...].astype(F32))
    cbuf[hl:hl + tm, :] = c_cur[...].astype(F32)
    cbuf[hl + tm:hl + tm + hl, :] = jnp.where(last, 0.0, cu_new[0:hl, :].astype(F32))
    for r0 in range(0, tm, CONV_ROWS):
        accb = jnp.zeros((CONV_ROWS, cb), F32)
        for j in range(kb):
            s = hl + r0 + j - kb // 2
            accb = accb + wb_ref[j:j + 1, :] * cbuf[s:s + CONV_ROWS, :]
        bact_ref[r0:r0 + CONV_ROWS, :] = (gb_cur[r0:r0 + CONV_ROWS, :].astype(F32) * accb).astype(BF16)

    a_tail[...] = a_cur[tm - hl:tm, :]
    a_cur[...] = a_new
    c_tail[...] = c_cur[tm - hl:tm, :]
    c_cur[...] = cu_new
    gb_cur[...] = gb_new


def _in_proj(x2, g, w_all, layer, wa, ba, lg, lb, wb, aw, bsz):
    t, d = x2.shape
    n = w_all.shape[2]
    s = t // bsz
    ka, ca = wa.shape
    kb, cb = wb.shape
    gw = ATT_OUT
    dils = tuple(dil for _, dil in DIL_GROUPS)
    tm = _tile(s, 512)
    nt = s // tm
    n_tiles = t // tm
    hl = CONV_HALO
    sub = V7X_F32_SUBLANES
    assert all(tm % (dil * V7X_BF16_SUBLANES) == 0 for dil in dils if dil > 1)
    assert (tm // 2) % CONV_ROWS == 0 and ka // 2 < hl and kb // 2 < hl and n == 2 * ca + 3 * cb + 3 * aw
    cur = lambda i: (jnp.minimum(i, n_tiles - 1), 0)
    behind = lambda i: (jnp.maximum(i - 1, 0), 0)
    fixed = lambda i: (0, 0)
    qkv_specs, qkv_shapes = [], []
    for dil in dils:
        if dil == 1:
            qkv_specs.append(pl.BlockSpec((tm, 3 * gw), cur))
            qkv_shapes.append(jax.ShapeDtypeStruct((t, 3 * gw), BF16))
        else:
            qkv_specs.append(pl.BlockSpec(
                (1, dil, tm // dil, 3 * gw),
                lambda i: (jnp.minimum(i, n_tiles - 1) // nt, 0, jnp.minimum(i, n_tiles - 1) % nt, 0)))
            qkv_shapes.append(jax.ShapeDtypeStruct((bsz, dil, s // dil, 3 * gw), BF16))
    span = tm // 2 + 2 * hl
    return pl.pallas_call(
        functools.partial(_in_proj_kernel, ca=ca, cb=cb, aw=aw, gw=gw, dils=dils, nt=nt, ka=ka, kb=kb),
        grid=(n_tiles + 1,),
        in_specs=[
            pl.BlockSpec((tm, d), cur),
            pl.BlockSpec((1, d), fixed),
            pl.BlockSpec((None, d, n), lambda i: (layer, 0, 0), pipeline_mode=pl.Buffered(1)),
            pl.BlockSpec((ka, sub, ca), lambda i: (0, 0, 0)), pl.BlockSpec((1, ca), fixed),
            pl.BlockSpec((1, ca), fixed), pl.BlockSpec((1, ca), fixed),
            pl.BlockSpec((kb, cb), fixed),
        ],
        out_specs=[
            pl.BlockSpec((tm, d), cur),
            pl.BlockSpec((tm, ca), behind),
            pl.BlockSpec((tm, cb), behind),
        ] + qkv_specs,
        out_shape=[
            jax.ShapeDtypeStruct((t, d), BF16),
            jax.ShapeDtypeStruct((t, ca), BF16),
            jax.ShapeDtypeStruct((t, cb), BF16),
        ] + qkv_shapes,
        scratch_shapes=[
            pltpu.VMEM((gw // V7X_LANES, tm, V7X_LANES), F32),
            pltpu.VMEM((hl, ca), BF16), pltpu.VMEM((tm, ca), BF16),
            pltpu.VMEM((hl, cb), BF16), pltpu.VMEM((tm, cb), BF16),
            pltpu.VMEM((tm, cb), BF16),
            pltpu.VMEM((sub, span, ca), F32),
            pltpu.VMEM((tm + 2 * hl, cb), F32),
        ],
        compiler_params=_params("arbitrary"),
        name="in_proj_conv",
    )(x2, g, w_all, jnp.broadcast_to(wa[:, None, :], (ka, sub, ca)), ba, lg, lb, wb)


ATT_QROWS = 128


def _attn_kernel(q_ref, kc, kp, kn, vc, vp, vn, o_ref, lse_ref, kbuf, vbuf, *, tq, sub_len, dil, half, slopes):
    i = pl.program_id(2)
    kbuf[0:half, :] = kp[0]
    kbuf[half:half + tq, :] = kc[0]
    kbuf[half + tq:half + tq + half, :] = kn[0]
    vbuf[0:half, :] = vp[0]
    vbuf[half:half + tq, :] = vc[0]
    vbuf[half + tq:half + tq + half, :] = vn[0]

    qr = ATT_QROWS
    kw = qr + 2 * half
    lanes = V7X_LANES
    heads_per_pair = lanes // HEAD_DIM
    rel = (lax.broadcasted_iota(jnp.int32, (qr, kw), 1) - half
           - lax.broadcasted_iota(jnp.int32, (qr, kw), 0))
    arel = jnp.abs(rel)
    band = arel <= half
    dist = (dil * arel).astype(F32)
    biases = [jnp.where(band, -slope * dist, NEG) for slope in slopes]
    lane_head = lax.broadcasted_iota(jnp.int32, (1, lanes), 1) // HEAD_DIM

    for o in range(0, tq, qr):
        kpos = i * tq + (o - half) + lax.broadcasted_iota(jnp.int32, (1, kw), 1)
        pen = jnp.where((kpos >= 0) & (kpos < sub_len), 0.0, NEG)
        for pair in range(q_ref.shape[2] // lanes):
            cols = slice(pair * lanes, (pair + 1) * lanes)
            qp = q_ref[0, o:o + qr, cols]
            kpair = kbuf[o:o + kw, cols]
            vpair = vbuf[o:o + kw, cols]
            out = None
            lse = None
            for hh in range(heads_per_pair):
                sel = lane_head == hh
                qm = jnp.where(sel, qp, jnp.zeros_like(qp))
                s = lax.dot_general(qm, kpair, (((1,), (1,)), ((), ())), preferred_element_type=F32)
                s = s + biases[pair * heads_per_pair + hh] + pen
                m = jnp.max(s, axis=-1, keepdims=True)
                p = jnp.exp(s - m)
                l = jnp.sum(p, axis=-1, keepdims=True)
                pv = _dot(p.astype(BF16), vpair) * (1.0 / l)
                hl = m + jnp.log(l)
                out = pv if out is None else jnp.where(sel, pv, out)
                lse = jnp.broadcast_to(hl, pv.shape) if lse is None else jnp.where(sel, hl, lse)
            o_ref[0, o:o + qr, cols] = out.astype(BF16)
            lse_ref[0, o:o + qr, cols] = lse


def _dilated_attention(qkv, g):
    bsz, dil, sub_len, w3 = qkv.shape
    window, gdil = DIL_GROUPS[g]
    assert gdil == dil
    half = window // (2 * dil)
    gw = w3 // 3
    assert sub_len % half == 0
    tq = _tile(sub_len, 512)
    assert tq % ATT_QROWS == 0 and tq % half == 0 and half % V7X_BF16_SUBLANES == 0
    nh = sub_len // half
    slopes = tuple(float(2.0 ** (-8.0 * (g * HEADS_PER_GROUP + h + 1) / N_HEADS)) for h in range(HEADS_PER_GROUP))

    def spec(rows, which, row_map):
        return pl.BlockSpec((1, None, rows, gw), lambda b, c, i: (b, c, row_map(i), which))

    cur = lambda i: i
    prev = lambda i: jnp.maximum(i * (tq // half) - 1, 0)
    nxt = lambda i: jnp.minimum((i + 1) * (tq // half), nh - 1)
    out_spec = pl.BlockSpec((1, None, tq, gw), lambda b, c, i: (b, c, i, 0))
    return pl.pallas_call(
        functools.partial(_attn_kernel, tq=tq, sub_len=sub_len, dil=dil, half=half, slopes=slopes),
        grid=(bsz, dil, sub_len // tq),
        in_specs=[
            spec(tq, 0, cur),
            spec(tq, 1, cur), spec(half, 1, prev), spec(half, 1, nxt),
            spec(tq, 2, cur), spec(half, 2, prev), spec(half, 2, nxt),
        ],
        out_specs=[out_spec, out_spec],
        out_shape=[jax.ShapeDtypeStruct((bsz, dil, sub_len, gw), BF16),
                   jax.ShapeDtypeStruct((bsz, dil, sub_len, gw), F32)],
        scratch_shapes=[pltpu.VMEM((tq + 2 * half, gw), BF16), pltpu.VMEM((tq + 2 * half, gw), BF16)],
        compiler_params=_params("parallel", "parallel", "parallel"),
        name=f"dilated_attention_g{g}",
    )(qkv, qkv, qkv, qkv, qkv, qkv, qkv)


def _natural_order(ref, nat_ref):
    _, dil, rows, w = ref.shape
    if dil == 1:
        return ref[0, 0].astype(F32)
    lanes = V7X_LANES
    for c in range(dil):
        for s in range(w // lanes):
            nat_ref[s, pl.ds(c, rows, stride=dil), :] = ref[0, c, :, s * lanes:(s + 1) * lanes].astype(F32)
    return jnp.concatenate([nat_ref[s] for s in range(w // lanes)], axis=1)


def _mixer_kernel(x_ref, h_ref, a_ref, b_ref, o0, o1, o2, l0, l1, l2, wg_ref, wbr_ref, bg_ref, wo_ref,
                  out_ref, att_ref, nat_ref):
    j = pl.program_id(1)

    @pl.when(j == 0)
    def _():
        ls = [_natural_order(l, nat_ref) for l in (l0, l1, l2)]
        m = jnp.maximum(jnp.maximum(ls[0], ls[1]), ls[2])
        es = [jnp.exp(l - m) for l in ls]
        inv = 1.0 / (es[0] + es[1] + es[2])
        num = None
        for e, o in zip(es, (o0, o1, o2)):
            term = e * _natural_order(o, nat_ref)
            num = term if num is None else num + term
        att_ref[...] = (num * inv).astype(BF16)

    tn = wo_ref.shape[0]
    ca, cb = a_ref.shape[1], b_ref.shape[1]

    def projected_mix():
        gates = jax.nn.sigmoid(_dot(h_ref[...], wg_ref[...]) + bg_ref[...])
        mix = (gates[:, 0:tn] * _dot(a_ref[...], wbr_ref[0:ca, :])
               + gates[:, tn:2 * tn] * _dot(b_ref[...], wbr_ref[ca:ca + cb, :])
               + gates[:, 2 * tn:3 * tn] * _dot(att_ref[...], wbr_ref[ca + cb:, :]))
        return _dot(mix.astype(BF16), wo_ref[...])

    @pl.when(j == 0)
    def _():
        out_ref[...] = x_ref[...] + projected_mix()

    @pl.when(j > 0)
    def _():
        out_ref[...] += projected_mix()


def _mixer(x2, h, a, b, os_, ls_, layer, w_gate, w_branch, b_gate, wo):
    t, d = x2.shape
    ca, cb = a.shape[1], b.shape[1]
    bsz, _, _, gw = os_[0].shape
    s = t // bsz
    tm = _tile(s, 512)
    nt = s // tm
    nj, tn = w_branch.shape[1], w_branch.shape[3]
    assert nj * tn == d and w_gate.shape[1:] == (nj, d, 3 * tn) and w_branch.shape[2] == ca + cb + gw
    row = lambda i, j: (i, 0)
    chunk = lambda i, j: (layer, j, 0, 0)
    att_specs = [pl.BlockSpec((1, o.shape[1], tm // o.shape[1], gw), lambda i, j: (i // nt, 0, i % nt, 0))
                 for o in os_]
    return pl.pallas_call(
        _mixer_kernel,
        grid=(t // tm, nj),
        in_specs=[
            pl.BlockSpec((tm, d), row), pl.BlockSpec((tm, d), row),
            pl.BlockSpec((tm, ca), row), pl.BlockSpec((tm, cb), row),
            *att_specs, *att_specs,
            pl.BlockSpec((None, None, d, 3 * tn), chunk),
            pl.BlockSpec((None, None, ca + cb + gw, tn), chunk),
            pl.BlockSpec((None, None, 1, 3 * tn), chunk),
            pl.BlockSpec((None, tn, d), lambda i, j: (layer, j, 0)),
        ],
        out_specs=pl.BlockSpec((tm, d), row),
        out_shape=jax.ShapeDtypeStruct((t, d), F32),
        scratch_shapes=[pltpu.VMEM((tm, gw), BF16), pltpu.VMEM((gw // V7X_LANES, tm, V7X_LANES), F32)],
        compiler_params=_params("parallel", "arbitrary"),
        name="gated_mixer",
    )(x2, h, a, b, *os_, *ls_, w_gate, w_branch, b_gate, wo)


def _ffn_kernel(x_ref, g_ref, wgu_ref, wd_ref, *rest, final):
    if final:
        fg_ref, out_ref, h_ref = rest
    else:
        out_ref, h_ref = rest
    j = pl.program_id(1)

    tf = wd_ref.shape[0]

    def down_projected():
        gu = _dot(h_ref[...], wgu_ref[...])
        gate = gu[:, 0:tf]
        f = gate * jax.nn.sigmoid(gate) * gu[:, tf:2 * tf]
        return _dot(f.astype(BF16), wd_ref[...])

    @pl.when(j == 0)
    def _():
        h_ref[...] = _rms_normed(x_ref[...], g_ref[...]).astype(BF16)
        out_ref[...] = x_ref[...] + down_projected()

    @pl.when(j > 0)
    def _():
        out_ref[...] += down_projected()

    if final:
        @pl.when(j == pl.num_programs(1) - 1)
        def _():
            out_ref[...] = _rms_normed(out_ref[...], fg_ref[...])


def _ffn(x2, g, layer, w_gate_up, wd, final_g=None):
    t, d = x2.shape
    nf, tf = w_gate_up.shape[1], w_gate_up.shape[3] // 2
    assert nf * tf == wd.shape[1]
    tm = _tile(t, 1024)
    row = lambda i, j: (i, 0)
    fixed = lambda i, j: (0, 0)
    final = final_g is not None
    in_specs = [
        pl.BlockSpec((tm, d), row), pl.BlockSpec((1, d), fixed),
        pl.BlockSpec((None, None, d, 2 * tf), lambda i, j: (layer, j, 0, 0)),
        pl.BlockSpec((None, tf, d), lambda i, j: (layer, j, 0)),
    ]
    args = [x2, g, w_gate_up, wd]
    if final:
        in_specs.append(pl.BlockSpec((1, d), fixed))
        args.append(final_g)
    return pl.pallas_call(
        functools.partial(_ffn_kernel, final=final),
        grid=(t // tm, nf),
        in_specs=in_specs,
        out_specs=pl.BlockSpec((tm, d), row),
        out_shape=jax.ShapeDtypeStruct((t, d), F32),
        scratch_shapes=[pltpu.VMEM((tm, d), BF16)],
        compiler_params=_params("parallel", "arbitrary"),
        name="swiglu_ffn_final" if final else "swiglu_ffn",
    )(*args)


def kernel(x, norm1_g, w_in, b_gate, conv_a_w, conv_a_b, ln_a_g, ln_a_b, w_a_out, conv_b_w, w_b_out,
           w_c_out, w_o, norm2_g, w_ffn_gate, w_ffn_up, w_ffn_down, final_g):
    bsz, s, d = x.shape
    depth = w_in.shape[0]
    ca, cb = conv_a_w.shape[2], conv_b_w.shape[2]
    n_proj = w_in.shape[2] - 3 * d
    aw = (n_proj - 2 * ca - 3 * cb) // 3
    dff = w_ffn_gate.shape[2]
    assert aw == N_HEADS * HEAD_DIM and w_c_out.shape[1] == ATT_OUT

    tn = _tile(d, MIXER_COLS)
    nj = d // tn
    tf = _tile(dff, FFN_COLS)
    gate_chunks = [[(0, n_proj + k * d + j * tn, tn) for k in range(3)] for j in range(nj)]
    w_proj, w_gate = _repack_bf16([w_in], [("plain", 0, 0, n_proj), ("chunked", gate_chunks)])
    b_gate_c = b_gate.reshape(depth, 3, nj, tn).transpose(0, 2, 1, 3).reshape(depth, nj, 1, 3 * tn)
    w_branch = _stack_rows_bf16([w_a_out, w_b_out, w_c_out], tn)
    wo = _cast_bf16(w_o)
    w_gate_up = _repack_bf16(
        [w_ffn_gate, w_ffn_up], [("chunked", [[(0, j * tf, tf), (1, j * tf, tf)] for j in range(dff // tf)])])[0]
    w_down = _cast_bf16(w_ffn_down)

    row = lambda v: v.reshape(1, -1)
    x2 = x.reshape(bsz * s, d)
    for l in range(depth):
        h, a_act, b_act, *qkvs = _in_proj(
            x2, row(norm1_g[l]), w_proj, l, conv_a_w[l], row(conv_a_b[l]), row(ln_a_g[l]), row(ln_a_b[l]),
            conv_b_w[l], aw, bsz)
        att = [_dilated_attention(qkv.reshape(bsz, dil, s // dil, 3 * ATT_OUT), g)
               for g, (qkv, (_, dil)) in enumerate(zip(qkvs, DIL_GROUPS))]
        x2 = _mixer(x2, h, a_act, b_act, [o for o, _ in att], [ls for _, ls in att],
                    l, w_gate, w_branch, b_gate_c, wo)
        x2 = _ffn(x2, row(norm2_g[l]), l, w_gate_up, w_down, row(final_g) if l == depth - 1 else None)
    return x2.reshape(bsz, s, d)
```

```python
import functools

import jax
import jax.numpy as jnp
from jax import lax
from jax.experimental import pallas as pl
from jax.experimental.pallas import tpu as pltpu

F32 = jnp.float32
BF16 = jnp.bfloat16

HEAD_DIM = 64
N_HEADS = 12
DIL_GROUPS = ((128, 1), (512, 4), (2048, 16))
HEADS_PER_GROUP = N_HEADS // len(DIL_GROUPS)
ATT_OUT = HEADS_PER_GROUP * HEAD_DIM
NORM_EPS = 1e-6
LN_EPS = 1e-5
NEG = -1e30

V7X_LANES = 128
V7X_F32_SUBLANES = 8
V7X_BF16_SUBLANES = 16
V7X_VMEM_BYTES = 64 * 1024 * 1024
VMEM_LIMIT_BYTES = V7X_VMEM_BYTES - 8 * 1024 * 1024

MIXER_COLS = 512
FFN_COLS = 512


def _params(*semantics):
    return pltpu.CompilerParams(dimension_semantics=semantics, vmem_limit_bytes=VMEM_LIMIT_BYTES)


def _tile(n, target):
    t = min(n, target)
    assert n % t == 0, (n, t)
    return t


def _rms_normed(x, g):
    ms = jnp.mean(x * x, axis=-1, keepdims=True)
    return x * lax.rsqrt(ms + NORM_EPS) * g


def _dot(a, b):
    return jnp.dot(a, b, preferred_element_type=F32)


CAST_BLOCK_BYTES = 4 * 1024 * 1024


def _repack_kernel(*refs, n_in, plans):
    ins, outs = refs[:n_in], refs[n_in:]
    for o_ref, plan in zip(outs, plans):
        if plan[0] == "plain":
            _, k, c0, width = plan
            o_ref[0] = ins[k][0, :, c0:c0 + width].astype(BF16)
        else:
            for j, pieces in enumerate(plan[1]):
                off = 0
                for k, c0, width in pieces:
                    o_ref[0, j, :, off:off + width] = ins[k][0, :, c0:c0 + width].astype(BF16)
                    off += width


def _repack_bf16(ws, plans):
    nl, r, _ = ws[0].shape
    assert all(w.shape[:2] == (nl, r) for w in ws)
    row_bytes = 4 * sum(w.shape[2] for w in ws)
    rb = r
    while rb * row_bytes > CAST_BLOCK_BYTES and rb % (2 * V7X_BF16_SUBLANES) == 0:
        rb //= 2
    out_specs, out_shapes = [], []
    for plan in plans:
        if plan[0] == "plain":
            width = plan[3]
            out_specs.append(pl.BlockSpec((1, rb, width), lambda l, i: (l, i, 0)))
            out_shapes.append(jax.ShapeDtypeStruct((nl, r, width), BF16))
        else:
            nj = len(plan[1])
            width = sum(p[2] for p in plan[1][0])
            assert all(sum(p[2] for p in pieces) == width for pieces in plan[1])
            out_specs.append(pl.BlockSpec((1, nj, rb, width), lambda l, i: (l, 0, i, 0)))
            out_shapes.append(jax.ShapeDtypeStruct((nl, nj, r, width), BF16))
    outs = pl.pallas_call(
        functools.partial(_repack_kernel, n_in=len(ws), plans=plans),
        grid=(nl, r // rb),
        in_specs=[pl.BlockSpec((1, rb, w.shape[2]), lambda l, i: (l, i, 0)) for w in ws],
        out_specs=out_specs,
        out_shape=out_shapes,
        compiler_params=_params("parallel", "parallel"),
        name="repack_bf16",
    )(*ws)
    return outs


def _cast_bf16(w):
    return _repack_bf16([w], [("plain", 0, 0, w.shape[2])])[0]


def _stack_rows_kernel(*refs, tn):
    ins, o_ref = refs[:-1], refs[-1]
    for j in range(o_ref.shape[1]):
        off = 0
        for w_ref in ins:
            rows = w_ref.shape[1]
            o_ref[0, j, off:off + rows, :] = w_ref[0, :, j * tn:(j + 1) * tn].astype(BF16)
            off += rows


def _stack_rows_bf16(ws, tn):
    nl, _, c = ws[0].shape
    rows = sum(w.shape[1] for w in ws)
    return pl.pallas_call(
        functools.partial(_stack_rows_kernel, tn=tn),
        grid=(nl,),
        in_specs=[pl.BlockSpec((1, w.shape[1], c), lambda l: (l, 0, 0)) for w in ws],
        out_specs=pl.BlockSpec((1, c // tn, rows, tn), lambda l: (l, 0, 0, 0)),
        out_shape=jax.ShapeDtypeStruct((nl, c // tn, rows, tn), BF16),
        compiler_params=_params("parallel"),
        name="stack_rows_bf16",
    )(*ws)


CONV_HALO = V7X_BF16_SUBLANES
CONV_ROWS = 32


def _shifted_copies(abuf):
    sub = V7X_F32_SUBLANES
    n_sh = abuf.shape[1] - sub
    for s in range(1, sub):
        abuf[s, 0:n_sh, :] = abuf[0, s:s + n_sh, :]


def _conv_a_chunk(abuf, wa_ref, ba_ref, lg_ref, lb_ref, out_ref, r0, out_row0, ka):
    sub = V7X_F32_SUBLANES
    acc = jnp.zeros((CONV_ROWS // sub, sub, abuf.shape[2]), F32)
    for j in range(ka):
        s = CONV_HALO + r0 + j - ka // 2
        al = s // sub * sub
        acc = acc + wa_ref[j] * abuf[s % sub, al:al + CONV_ROWS, :].reshape(acc.shape)
    acc = acc.reshape(CONV_ROWS, abuf.shape[2]) + ba_ref[...]
    mu = jnp.mean(acc, axis=-1, keepdims=True)
    cen = acc - mu
    var = jnp.mean(cen * cen, axis=-1, keepdims=True)
    y = cen * lax.rsqrt(var + LN_EPS) * lg_ref[...] + lb_ref[...]
    out_ref[out_row0 + r0:out_row0 + r0 + CONV_ROWS, :] = (y * jax.nn.sigmoid(y)).astype(BF16)


def _in_proj_kernel(x_ref, g_ref, w_ref, wa_ref, ba_ref, lg_ref, lb_ref, wb_ref,
                    h_ref, aact_ref, bact_ref, *rest, ca, cb, aw, gw, dils, nt, ka, kb):
    ng = len(dils)
    qkv_refs = rest[:ng]
    zs, a_tail, a_cur, c_tail, c_cur, gb_cur, abuf, cbuf = rest[ng:]
    i = pl.program_id(0)
    tm = x_ref.shape[0]
    hl = CONV_HALO
    lanes = V7X_LANES

    @pl.when(i == 0)
    def _():
        for ref in (a_tail, a_cur, c_tail, c_cur, gb_cur):
            ref[...] = jnp.zeros(ref.shape, ref.dtype)

    h = _rms_normed(x_ref[...], g_ref[...]).astype(BF16)
    h_ref[...] = h

    def proj(c0, c1):
        return _dot(h, w_ref[:, c0:c1])

    a_new = (proj(0, ca) * jax.nn.sigmoid(proj(ca, 2 * ca))).astype(BF16)
    o = 2 * ca
    gb_new = proj(o, o + cb).astype(BF16)
    cu_new = (proj(o + cb, o + 2 * cb) * proj(o + 2 * cb, o + 3 * cb)).astype(BF16)
    o += 3 * cb
    for g, dil in enumerate(dils):
        ref = qkv_refs[g]
        for n in range(3):
            z = proj(o + n * aw + g * gw, o + n * aw + (g + 1) * gw)
            if n == 0:
                z = z * (HEAD_DIM ** -0.5)
            if dil == 1:
                ref[:, n * gw:(n + 1) * gw] = z.astype(BF16)
                continue
            for s in range(gw // lanes):
                zs[s] = z[:, s * lanes:(s + 1) * lanes]
            for c in range(dil):
                for s in range(gw // lanes):
                    ref[0, c, :, n * gw + s * lanes:n * gw + (s + 1) * lanes] = (
                        zs[s, pl.ds(c, tm // dil, stride=dil), :].astype(BF16))

    p = lax.rem(i + nt - 1, nt)
    first, last = p == 0, p == nt - 1
    half_rows = tm // 2
    span = half_rows + 2 * hl
    abuf[0, 0:hl, :] = jnp.where(first, 0.0, a_tail[...].astype(F32))
    abuf[0, hl:span, :] = a_cur[0:span - hl, :].astype(F32)
    _shifted_copies(abuf)
    for r0 in range(0, half_rows, CONV_ROWS):
        _conv_a_chunk(abuf, wa_ref, ba_ref, lg_ref, lb_ref, aact_ref, r0, 0, ka)
    abuf[0, 0:span - hl, :] = a_cur[half_rows - hl:tm, :].astype(F32)
    abuf[0, span - hl:span, :] = jnp.where(last, 0.0, a_new[0:hl, :].astype(F32))
    _shifted_copies(abuf)
    for r0 in range(0, half_rows, CONV_ROWS):
        _conv_a_chunk(abuf, wa_ref, ba_ref, lg_ref, lb_ref, aact_ref, r0, half_rows, ka)

    cbuf[0:hl, :] = jnp.where(first, 0.0, c_tail[...].astype(F32))
    cbuf[hl:hl + tm, :] = c_cur[...].astype(F32)
    cbuf[hl + tm:hl + tm + hl, :] = jnp.where(last, 0.0, cu_new[0:hl, :].astype(F32))
    for r0 in range(0, tm, CONV_ROWS):
        accb = jnp.zeros((CONV_ROWS, cb), F32)
        for j in range(kb):
            s = hl + r0 + j - kb // 2
            accb = accb + wb_ref[j:j + 1, :] * cbuf[s:s + CONV_ROWS, :]
        bact_ref[r0:r0 + CONV_ROWS, :] = (gb_cur[r0:r0 + CONV_ROWS, :].astype(F32) * accb).astype(BF16)

    a_tail[...] = a_cur[tm - hl:tm, :]
    a_cur[...] = a_new
    c_tail[...] = c_cur[tm - hl:tm, :]
    c_cur[...] = cu_new
    gb_cur[...] = gb_new


def _in_proj(x2, g, w_all, layer, wa, ba, lg, lb, wb, aw, bsz):
    t, d = x2.shape
    n = w_all.shape[2]
    s = t // bsz
    ka, ca = wa.shape
    kb, cb = wb.shape
    gw = ATT_OUT
    dils = tuple(dil for _, dil in DIL_GROUPS)
    tm = _tile(s, 512)
    nt = s // tm
    n_tiles = t // tm
    hl = CONV_HALO
    sub = V7X_F32_SUBLANES
    assert all(tm % (dil * V7X_BF16_SUBLANES) == 0 for dil in dils if dil > 1)
    assert (tm // 2) % CONV_ROWS == 0 and ka // 2 < hl and kb // 2 < hl and n == 2 * ca + 3 * cb + 3 * aw
    cur = lambda i: (jnp.minimum(i, n_tiles - 1), 0)
    behind = lambda i: (jnp.maximum(i - 1, 0), 0)
    fixed = lambda i: (0, 0)
    qkv_specs, qkv_shapes = [], []
    for dil in dils:
        if dil == 1:
            qkv_specs.append(pl.BlockSpec((tm, 3 * gw), cur))
            qkv_shapes.append(jax.ShapeDtypeStruct((t, 3 * gw), BF16))
        else:
            qkv_specs.append(pl.BlockSpec(
                (1, dil, tm // dil, 3 * gw),
                lambda i: (jnp.minimum(i, n_tiles - 1) // nt, 0, jnp.minimum(i, n_tiles - 1) % nt, 0)))
            qkv_shapes.append(jax.ShapeDtypeStruct((bsz, dil, s // dil, 3 * gw), BF16))
    span = tm // 2 + 2 * hl
    return pl.pallas_call(
        functools.partial(_in_proj_kernel, ca=ca, cb=cb, aw=aw, gw=gw, dils=dils, nt=nt, ka=ka, kb=kb),
        grid=(n_tiles + 1,),
        in_specs=[
            pl.BlockSpec((tm, d), cur),
            pl.BlockSpec((1, d), fixed),
            pl.BlockSpec((None, d, n), lambda i: (layer, 0, 0), pipeline_mode=pl.Buffered(1)),
            pl.BlockSpec((ka, sub, ca), lambda i: (0, 0, 0)), pl.BlockSpec((1, ca), fixed),
            pl.BlockSpec((1, ca), fixed), pl.BlockSpec((1, ca), fixed),
            pl.BlockSpec((kb, cb), fixed),
        ],
        out_specs=[
            pl.BlockSpec((tm, d), cur),
            pl.BlockSpec((tm, ca), behind),
            pl.BlockSpec((tm, cb), behind),
        ] + qkv_specs,
        out_shape=[
            jax.ShapeDtypeStruct((t, d), BF16),
            jax.ShapeDtypeStruct((t, ca), BF16),
            jax.ShapeDtypeStruct((t, cb), BF16),
        ] + qkv_shapes,
        scratch_shapes=[
            pltpu.VMEM((gw // V7X_LANES, tm, V7X_LANES), F32),
            pltpu.VMEM((hl, ca), BF16), pltpu.VMEM((tm, ca), BF16),
            pltpu.VMEM((hl, cb), BF16), pltpu.VMEM((tm, cb), BF16),
            pltpu.VMEM((tm, cb), BF16),
            pltpu.VMEM((sub, span, ca), F32),
            pltpu.VMEM((tm + 2 * hl, cb), F32),
        ],
        compiler_params=_params("arbitrary"),
        name="in_proj_conv",
    )(x2, g, w_all, jnp.broadcast_to(wa[:, None, :], (ka, sub, ca)), ba, lg, lb, wb)


ATT_QROWS = 128


def _attn_kernel(q_ref, kc, kp, kn, vc, vp, vn, o_ref, lse_ref, kbuf, vbuf, *, tq, sub_len, dil, half, slopes):
    i = pl.program_id(2)
    kbuf[0:half, :] = kp[0]
    kbuf[half:half + tq, :] = kc[0]
    kbuf[half + tq:half + tq + half, :] = kn[0]
    vbuf[0:half, :] = vp[0]
    vbuf[half:half + tq, :] = vc[0]
    vbuf[half + tq:half + tq + half, :] = vn[0]

    qr = ATT_QROWS
    kw = qr + 2 * half
    lanes = V7X_LANES
    heads_per_pair = lanes // HEAD_DIM
    rel = (lax.broadcasted_iota(jnp.int32, (qr, kw), 1) - half
           - lax.broadcasted_iota(jnp.int32, (qr, kw), 0))
    arel = jnp.abs(rel)
    band = arel <= half
    dist = (dil * arel).astype(F32)
    biases = [jnp.where(band, -slope * dist, NEG) for slope in slopes]
    lane_head = lax.broadcasted_iota(jnp.int32, (1, lanes), 1) // HEAD_DIM

    for o in range(0, tq, qr):
        at_edge = o == 0 or o + qr == tq
        if at_edge:
            kpos = i * tq + (o - half) + lax.broadcasted_iota(jnp.int32, (1, kw), 1)
            pen = jnp.where((kpos >= 0) & (kpos < sub_len), 0.0, NEG)
        for pair in range(q_ref.shape[2] // lanes):
            cols = slice(pair * lanes, (pair + 1) * lanes)
            qp = q_ref[0, o:o + qr, cols]
            kpair = kbuf[o:o + kw, cols]
            vpair = vbuf[o:o + kw, cols]
            out = None
            lse = None
            for hh in range(heads_per_pair):
                sel = lane_head == hh
                qm = jnp.where(sel, qp, jnp.zeros_like(qp))
                s = lax.dot_general(qm, kpair, (((1,), (1,)), ((), ())), preferred_element_type=F32)
                s = s + biases[pair * heads_per_pair + hh]
                if at_edge:
                    s = s + pen
                m = jnp.max(s, axis=-1, keepdims=True)
                p = jnp.exp(s - m)
                l = jnp.sum(p, axis=-1, keepdims=True)
                pv = _dot(p.astype(BF16), vpair) * (1.0 / l)
                hl = m + jnp.log(l)
                out = pv if out is None else jnp.where(sel, pv, out)
                lse = jnp.broadcast_to(hl, pv.shape) if lse is None else jnp.where(sel, hl, lse)
            o_ref[0, o:o + qr, cols] = out.astype(BF16)
            lse_ref[0, o:o + qr, cols] = lse


def _dilated_attention(qkv, g):
    bsz, dil, sub_len, w3 = qkv.shape
    window, gdil = DIL_GROUPS[g]
    assert gdil == dil
    half = window // (2 * dil)
    gw = w3 // 3
    assert sub_len % half == 0
    tq = _tile(sub_len, 1024)
    assert tq % ATT_QROWS == 0 and tq % half == 0 and half % V7X_BF16_SUBLANES == 0
    nh = sub_len // half
    slopes = tuple(float(2.0 ** (-8.0 * (g * HEADS_PER_GROUP + h + 1) / N_HEADS)) for h in range(HEADS_PER_GROUP))

    def spec(rows, which, row_map):
        return pl.BlockSpec((1, None, rows, gw), lambda b, c, i: (b, c, row_map(i), which))

    cur = lambda i: i
    prev = lambda i: jnp.maximum(i * (tq // half) - 1, 0)
    nxt = lambda i: jnp.minimum((i + 1) * (tq // half), nh - 1)
    out_spec = pl.BlockSpec((1, None, tq, gw), lambda b, c, i: (b, c, i, 0))
    return pl.pallas_call(
        functools.partial(_attn_kernel, tq=tq, sub_len=sub_len, dil=dil, half=half, slopes=slopes),
        grid=(bsz, dil, sub_len // tq),
        in_specs=[
            spec(tq, 0, cur),
            spec(tq, 1, cur), spec(half, 1, prev), spec(half, 1, nxt),
            spec(tq, 2, cur), spec(half, 2, prev), spec(half, 2, nxt),
        ],
        out_specs=[out_spec, out_spec],
        out_shape=[jax.ShapeDtypeStruct((bsz, dil, sub_len, gw), BF16),
                   jax.ShapeDtypeStruct((bsz, dil, sub_len, gw), F32)],
        scratch_shapes=[pltpu.VMEM((tq + 2 * half, gw), BF16), pltpu.VMEM((tq + 2 * half, gw), BF16)],
        compiler_params=_params("parallel", "parallel", "parallel"),
        name=f"dilated_attention_g{g}",
    )(qkv, qkv, qkv, qkv, qkv, qkv, qkv)


def _natural_order(ref, nat_ref):
    _, dil, rows, w = ref.shape
    if dil == 1:
        return ref[0, 0].astype(F32)
    lanes = V7X_LANES
    for c in range(dil):
        for s in range(w // lanes):
            nat_ref[s, pl.ds(c, rows, stride=dil), :] = ref[0, c, :, s * lanes:(s + 1) * lanes].astype(F32)
    return jnp.concatenate([nat_ref[s] for s in range(w // lanes)], axis=1)


def _mixer_kernel(x_ref, h_ref, a_ref, b_ref, o0, o1, o2, l0, l1, l2, wg_ref, wbr_ref, bg_ref, wo_ref,
                  out_ref, att_ref, nat_ref):
    j = pl.program_id(1)

    @pl.when(j == 0)
    def _():
        ls = [_natural_order(l, nat_ref) for l in (l0, l1, l2)]
        m = jnp.maximum(jnp.maximum(ls[0], ls[1]), ls[2])
        es = [jnp.exp(l - m) for l in ls]
        inv = 1.0 / (es[0] + es[1] + es[2])
        num = None
        for e, o in zip(es, (o0, o1, o2)):
            term = e * _natural_order(o, nat_ref)
            num = term if num is None else num + term
        att_ref[...] = (num * inv).astype(BF16)

    tn = wo_ref.shape[0]
    ca, cb = a_ref.shape[1], b_ref.shape[1]

    def projected_mix():
        gates = jax.nn.sigmoid(_dot(h_ref[...], wg_ref[...]) + bg_ref[...])
        mix = (gates[:, 0:tn] * _dot(a_ref[...], wbr_ref[0:ca, :])
               + gates[:, tn:2 * tn] * _dot(b_ref[...], wbr_ref[ca:ca + cb, :])
               + gates[:, 2 * tn:3 * tn] * _dot(att_ref[...], wbr_ref[ca + cb:, :]))
        return _dot(mix.astype(BF16), wo_ref[...])

    @pl.when(j == 0)
    def _():
        out_ref[...] = x_ref[...] + projected_mix()

    @pl.when(j > 0)
    def _():
        out_ref[...] += projected_mix()


def _mixer(x2, h, a, b, os_, ls_, layer, w_gate, w_branch, b_gate, wo):
    t, d = x2.shape
    ca, cb = a.shape[1], b.shape[1]
    bsz, _, _, gw = os_[0].shape
    s = t // bsz
    tm = _tile(s, 512)
    nt = s // tm
    nj, tn = w_branch.shape[1], w_branch.shape[3]
    assert nj * tn == d and w_gate.shape[1:] == (nj, d, 3 * tn) and w_branch.shape[2] == ca + cb + gw
    row = lambda i, j: (i, 0)
    chunk = lambda i, j: (layer, j, 0, 0)
    att_specs = [pl.BlockSpec((1, o.shape[1], tm // o.shape[1], gw), lambda i, j: (i // nt, 0, i % nt, 0))
                 for o in os_]
    return pl.pallas_call(
        _mixer_kernel,
        grid=(t // tm, nj),
        in_specs=[
            pl.BlockSpec((tm, d), row), pl.BlockSpec((tm, d), row),
            pl.BlockSpec((tm, ca), row), pl.BlockSpec((tm, cb), row),
            *att_specs, *att_specs,
            pl.BlockSpec((None, None, d, 3 * tn), chunk),
            pl.BlockSpec((None, None, ca + cb + gw, tn), chunk),
            pl.BlockSpec((None, None, 1, 3 * tn), chunk),
            pl.BlockSpec((None, tn, d), lambda i, j: (layer, j, 0)),
        ],
        out_specs=pl.BlockSpec((tm, d), row),
        out_shape=jax.ShapeDtypeStruct((t, d), F32),
        scratch_shapes=[pltpu.VMEM((tm, gw), BF16), pltpu.VMEM((gw // V7X_LANES, tm, V7X_LANES), F32)],
        compiler_params=_params("parallel", "arbitrary"),
        name="gated_mixer",
    )(x2, h, a, b, *os_, *ls_, w_gate, w_branch, b_gate, wo)


def _ffn_kernel(x_ref, g_ref, wgu_ref, wd_ref, *rest, final):
    if final:
        fg_ref, out_ref, h_ref = rest
    else:
        out_ref, h_ref = rest
    j = pl.program_id(1)

    tf = wd_ref.shape[0]

    def down_projected():
        gu = _dot(h_ref[...], wgu_ref[...])
        gate = gu[:, 0:tf]
        f = gate * jax.nn.sigmoid(gate) * gu[:, tf:2 * tf]
        return _dot(f.astype(BF16), wd_ref[...])

    @pl.when(j == 0)
    def _():
        h_ref[...] = _rms_normed(x_ref[...], g_ref[...]).astype(BF16)
        out_ref[...] = x_ref[...] + down_projected()

    @pl.when(j > 0)
    def _():
        out_ref[...] += down_projected()

    if final:
        @pl.when(j == pl.num_programs(1) - 1)
        def _():
            out_ref[...] = _rms_normed(out_ref[...], fg_ref[...])


def _ffn(x2, g, layer, w_gate_up, wd, final_g=None):
    t, d = x2.shape
    nf, tf = w_gate_up.shape[1], w_gate_up.shape[3] // 2
    assert nf * tf == wd.shape[1]
    tm = _tile(t, 1024)
    row = lambda i, j: (i, 0)
    fixed = lambda i, j: (0, 0)
    final = final_g is not None
    in_specs = [
        pl.BlockSpec((tm, d), row), pl.BlockSpec((1, d), fixed),
        pl.BlockSpec((None, None, d, 2 * tf), lambda i, j: (layer, j, 0, 0)),
        pl.BlockSpec((None, tf, d), lambda i, j: (layer, j, 0)),
    ]
    args = [x2, g, w_gate_up, wd]
    if final:
        in_specs.append(pl.BlockSpec((1, d), fixed))
        args.append(final_g)
    return pl.pallas_call(
        functools.partial(_ffn_kernel, final=final),
        grid=(t // tm, nf),
        in_specs=in_specs,
        out_specs=pl.BlockSpec((tm, d), row),
        out_shape=jax.ShapeDtypeStruct((t, d), F32),
        scratch_shapes=[pltpu.VMEM((tm, d), BF16)],
        compiler_params=_params("parallel", "arbitrary"),
        name="swiglu_ffn_final" if final else "swiglu_ffn",
    )(*args)


def kernel(x, norm1_g, w_in, b_gate, conv_a_w, conv_a_b, ln_a_g, ln_a_b, w_a_out, conv_b_w, w_b_out,
           w_c_out, w_o, norm2_g, w_ffn_gate, w_ffn_up, w_ffn_down, final_g):
    bsz, s, d = x.shape
    depth = w_in.shape[0]
    ca, cb = conv_a_w.shape[2], conv_b_w.shape[2]
    n_proj = w_in.shape[2] - 3 * d
    aw = (n_proj - 2 * ca - 3 * cb) // 3
    dff = w_ffn_gate.shape[2]
    assert aw == N_HEADS * HEAD_DIM and w_c_out.shape[1] == ATT_OUT

    tn = _tile(d, MIXER_COLS)
    nj = d // tn
    tf = _tile(dff, FFN_COLS)
    gate_chunks = [[(0, n_proj + k * d + j * tn, tn) for k in range(3)] for j in range(nj)]
    w_proj, w_gate = _repack_bf16([w_in], [("plain", 0, 0, n_proj), ("chunked", gate_chunks)])
    b_gate_c = b_gate.reshape(depth, 3, nj, tn).transpose(0, 2, 1, 3).reshape(depth, nj, 1, 3 * tn)
    w_branch = _stack_rows_bf16([w_a_out, w_b_out, w_c_out], tn)
    wo = _cast_bf16(w_o)
    w_gate_up = _repack_bf16(
        [w_ffn_gate, w_ffn_up], [("chunked", [[(0, j * tf, tf), (1, j * tf, tf)] for j in range(dff // tf)])])[0]
    w_down = _cast_bf16(w_ffn_down)

    row = lambda v: v.reshape(1, -1)
    x2 = x.reshape(bsz * s, d)
    for l in range(depth):
        h, a_act, b_act, *qkvs = _in_proj(
            x2, row(norm1_g[l]), w_proj, l, conv_a_w[l], row(conv_a_b[l]), row(ln_a_g[l]), row(ln_a_b[l]),
            conv_b_w[l], aw, bsz)
        att = [_dilated_attention(qkv.reshape(bsz, dil, s // dil, 3 * ATT_OUT), g)
               for g, (qkv, (_, dil)) in enumerate(zip(qkvs, DIL_GROUPS))]
        x2 = _mixer(x2, h, a_act, b_act, [o for o, _ in att], [ls for _, ls in att],
                    l, w_gate, w_branch, b_gate_c, wo)
        x2 = _ffn(x2, row(norm2_g[l]), l, w_gate_up, w_down, row(final_g) if l == depth - 1 else None)
    return x2.reshape(bsz, s, d)
```

```python
import functools

import jax
import jax.numpy as jnp
from jax import lax
from jax.experimental import pallas as pl
from jax.experimental.pallas import tpu as pltpu

F32 = jnp.float32
BF16 = jnp.bfloat16

HEAD_DIM = 64
N_HEADS = 12
DIL_GROUPS = ((128, 1), (512, 4), (2048, 16))
HEADS_PER_GROUP = N_HEADS // len(DIL_GROUPS)
ATT_OUT = HEADS_PER_GROUP * HEAD_DIM
NORM_EPS = 1e-6
LN_EPS = 1e-5
NEG = -1e30
LOG2_E = 1.4426950408889634
LN_2 = 0.6931471805599453

V7X_LANES = 128
V7X_F32_SUBLANES = 8
V7X_BF16_SUBLANES = 16
V7X_VMEM_BYTES = 64 * 1024 * 1024
VMEM_LIMIT_BYTES = V7X_VMEM_BYTES - 8 * 1024 * 1024

MIXER_COLS = 512
FFN_COLS = 512


def _params(*semantics):
    return pltpu.CompilerParams(dimension_semantics=semantics, vmem_limit_bytes=VMEM_LIMIT_BYTES)


def _tile(n, target):
    t = min(n, target)
    assert n % t == 0, (n, t)
    return t


def _rms_normed(x, g):
    ms = jnp.mean(x * x, axis=-1, keepdims=True)
    return x * lax.rsqrt(ms + NORM_EPS) * g


def _dot(a, b):
    return jnp.dot(a, b, preferred_element_type=F32)


CAST_BLOCK_BYTES = 4 * 1024 * 1024


def _repack_kernel(*refs, n_in, plans):
    ins, outs = refs[:n_in], refs[n_in:]
    for o_ref, plan in zip(outs, plans):
        if plan[0] == "plain":
            _, k, c0, width = plan
            o_ref[0] = ins[k][0, :, c0:c0 + width].astype(BF16)
        else:
            for j, pieces in enumerate(plan[1]):
                off = 0
                for k, c0, width in pieces:
                    o_ref[0, j, :, off:off + width] = ins[k][0, :, c0:c0 + width].astype(BF16)
                    off += width


def _repack_bf16(ws, plans):
    nl, r, _ = ws[0].shape
    assert all(w.shape[:2] == (nl, r) for w in ws)
    row_bytes = 4 * sum(w.shape[2] for w in ws)
    rb = r
    while rb * row_bytes > CAST_BLOCK_BYTES and rb % (2 * V7X_BF16_SUBLANES) == 0:
        rb //= 2
    out_specs, out_shapes = [], []
    for plan in plans:
        if plan[0] == "plain":
            width = plan[3]
            out_specs.append(pl.BlockSpec((1, rb, width), lambda l, i: (l, i, 0)))
            out_shapes.append(jax.ShapeDtypeStruct((nl, r, width), BF16))
        else:
            nj = len(plan[1])
            width = sum(p[2] for p in plan[1][0])
            assert all(sum(p[2] for p in pieces) == width for pieces in plan[1])
            out_specs.append(pl.BlockSpec((1, nj, rb, width), lambda l, i: (l, 0, i, 0)))
            out_shapes.append(jax.ShapeDtypeStruct((nl, nj, r, width), BF16))
    outs = pl.pallas_call(
        functools.partial(_repack_kernel, n_in=len(ws), plans=plans),
        grid=(nl, r // rb),
        in_specs=[pl.BlockSpec((1, rb, w.shape[2]), lambda l, i: (l, i, 0)) for w in ws],
        out_specs=out_specs,
        out_shape=out_shapes,
        compiler_params=_params("parallel", "parallel"),
        name="repack_bf16",
    )(*ws)
    return outs


def _cast_bf16(w):
    return _repack_bf16([w], [("plain", 0, 0, w.shape[2])])[0]


def _stack_rows_kernel(*refs, tn):
    ins, o_ref = refs[:-1], refs[-1]
    for j in range(o_ref.shape[1]):
        off = 0
        for w_ref in ins:
            rows = w_ref.shape[1]
            o_ref[0, j, off:off + rows, :] = w_ref[0, :, j * tn:(j + 1) * tn].astype(BF16)
            off += rows


def _stack_rows_bf16(ws, tn):
    nl, _, c = ws[0].shape
    rows = sum(w.shape[1] for w in ws)
    return pl.pallas_call(
        functools.partial(_stack_rows_kernel, tn=tn),
        grid=(nl,),
        in_specs=[pl.BlockSpec((1, w.shape[1], c), lambda l: (l, 0, 0)) for w in ws],
        out_specs=pl.BlockSpec((1, c // tn, rows, tn), lambda l: (l, 0, 0, 0)),
        out_shape=jax.ShapeDtypeStruct((nl, c // tn, rows, tn), BF16),
        compiler_params=_params("parallel"),
        name="stack_rows_bf16",
    )(*ws)


CONV_HALO = V7X_BF16_SUBLANES
CONV_ROWS = 32


def _shifted_copies(abuf):
    sub = V7X_F32_SUBLANES
    n_sh = abuf.shape[1] - sub
    for s in range(1, sub):
        abuf[s, 0:n_sh, :] = abuf[0, s:s + n_sh, :]


def _conv_a_chunk(abuf, wa_ref, ba_ref, lg_ref, lb_ref, out_ref, r0, out_row0, ka):
    sub = V7X_F32_SUBLANES
    acc = jnp.zeros((CONV_ROWS // sub, sub, abuf.shape[2]), F32)
    for j in range(ka):
        s = CONV_HALO + r0 + j - ka // 2
        al = s // sub * sub
        acc = acc + wa_ref[j] * abuf[s % sub, al:al + CONV_ROWS, :].reshape(acc.shape)
    acc = acc.reshape(CONV_ROWS, abuf.shape[2]) + ba_ref[...]
    mu = jnp.mean(acc, axis=-1, keepdims=True)
    cen = acc - mu
    var = jnp.mean(cen * cen, axis=-1, keepdims=True)
    y = cen * lax.rsqrt(var + LN_EPS) * lg_ref[...] + lb_ref[...]
    out_ref[out_row0 + r0:out_row0 + r0 + CONV_ROWS, :] = (y * jax.nn.sigmoid(y)).astype(BF16)


def _in_proj_kernel(x_ref, g_ref, w_ref, wa_ref, ba_ref, lg_ref, lb_ref, wb_ref,
                    h_ref, aact_ref, bact_ref, *rest, ca, cb, aw, gw, dils, nt, ka, kb):
    ng = len(dils)
    qkv_refs = rest[:ng]
    zs, a_tail, a_cur, c_tail, c_cur, gb_cur, abuf, cbuf = rest[ng:]
    i = pl.program_id(0)
    tm = x_ref.shape[0]
    hl = CONV_HALO
    lanes = V7X_LANES

    @pl.when(i == 0)
    def _():
        for ref in (a_tail, a_cur, c_tail, c_cur, gb_cur):
            ref[...] = jnp.zeros(ref.shape, ref.dtype)

    h = _rms_normed(x_ref[...], g_ref[...]).astype(BF16)
    h_ref[...] = h

    def proj(c0, c1):
        return _dot(h, w_ref[:, c0:c1])

    a_new = (proj(0, ca) * jax.nn.sigmoid(proj(ca, 2 * ca))).astype(BF16)
    o = 2 * ca
    gb_new = proj(o, o + cb).astype(BF16)
    cu_new = (proj(o + cb, o + 2 * cb) * proj(o + 2 * cb, o + 3 * cb)).astype(BF16)
    o += 3 * cb
    for g, dil in enumerate(dils):
        ref = qkv_refs[g]
        for n in range(3):
            z = proj(o + n * aw + g * gw, o + n * aw + (g + 1) * gw)
            if n == 0:
                z = z * (HEAD_DIM ** -0.5 * LOG2_E)
            if dil == 1:
                ref[:, n * gw:(n + 1) * gw] = z.astype(BF16)
                continue
            for s in range(gw // lanes):
                zs[s] = z[:, s * lanes:(s + 1) * lanes]
            for c in range(dil):
                for s in range(gw // lanes):
                    ref[0, c, :, n * gw + s * lanes:n * gw + (s + 1) * lanes] = (
                        zs[s, pl.ds(c, tm // dil, stride=dil), :].astype(BF16))

    p = lax.rem(i + nt - 1, nt)
    first, last = p == 0, p == nt - 1
    half_rows = tm // 2
    span = half_rows + 2 * hl
    abuf[0, 0:hl, :] = jnp.where(first, 0.0, a_tail[...].astype(F32))
    abuf[0, hl:span, :] = a_cur[0:span - hl, :].astype(F32)
    _shifted_copies(abuf)
    for r0 in range(0, half_rows, CONV_ROWS):
        _conv_a_chunk(abuf, wa_ref, ba_ref, lg_ref, lb_ref, aact_ref, r0, 0, ka)
    abuf[0, 0:span - hl, :] = a_cur[half_rows - hl:tm, :].astype(F32)
    abuf[0, span - hl:span, :] = jnp.where(last, 0.0, a_new[0:hl, :].astype(F32))
    _shifted_copies(abuf)
    for r0 in range(0, half_rows, CONV_ROWS):
        _conv_a_chunk(abuf, wa_ref, ba_ref, lg_ref, lb_ref, aact_ref, r0, half_rows, ka)

    cbuf[0:hl, :] = jnp.where(first, 0.0, c_tail[...].astype(F32))
    cbuf[hl:hl + tm, :] = c_cur[...].astype(F32)
    cbuf[hl + tm:hl + tm + hl, :] = jnp.where(last, 0.0, cu_new[0:hl, :].astype(F32))
    for r0 in range(0, tm, CONV_ROWS):
        accb = jnp.zeros((CONV_ROWS, cb), F32)
        for j in range(kb):
            s = hl + r0 + j - kb // 2
            accb = accb + wb_ref[j:j + 1, :] * cbuf[s:s + CONV_ROWS, :]
        bact_ref[r0:r0 + CONV_ROWS, :] = (gb_cur[r0:r0 + CONV_ROWS, :].astype(F32) * accb).astype(BF16)

    a_tail[...] = a_cur[tm - hl:tm, :]
    a_cur[...] = a_new
    c_tail[...] = c_cur[tm - hl:tm, :]
    c_cur[...] = cu_new
    gb_cur[...] = gb_new


def _in_proj(x2, g, w_all, layer, wa, ba, lg, lb, wb, aw, bsz):
    t, d = x2.shape
    n = w_all.shape[2]
    s = t // bsz
    ka, ca = wa.shape
    kb, cb = wb.shape
    gw = ATT_OUT
    dils = tuple(dil for _, dil in DIL_GROUPS)
    tm = _tile(s, 512)
    nt = s // tm
    n_tiles = t // tm
    hl = CONV_HALO
    sub = V7X_F32_SUBLANES
    assert all(tm % (dil * V7X_BF16_SUBLANES) == 0 for dil in dils if dil > 1)
    assert (tm // 2) % CONV_ROWS == 0 and ka // 2 < hl and kb // 2 < hl and n == 2 * ca + 3 * cb + 3 * aw
    cur = lambda i: (jnp.minimum(i, n_tiles - 1), 0)
    behind = lambda i: (jnp.maximum(i - 1, 0), 0)
    fixed = lambda i: (0, 0)
    qkv_specs, qkv_shapes = [], []
    for dil in dils:
        if dil == 1:
            qkv_specs.append(pl.BlockSpec((tm, 3 * gw), cur))
            qkv_shapes.append(jax.ShapeDtypeStruct((t, 3 * gw), BF16))
        else:
            qkv_specs.append(pl.BlockSpec(
                (1, dil, tm // dil, 3 * gw),
                lambda i: (jnp.minimum(i, n_tiles - 1) // nt, 0, jnp.minimum(i, n_tiles - 1) % nt, 0)))
            qkv_shapes.append(jax.ShapeDtypeStruct((bsz, dil, s // dil, 3 * gw), BF16))
    span = tm // 2 + 2 * hl
    return pl.pallas_call(
        functools.partial(_in_proj_kernel, ca=ca, cb=cb, aw=aw, gw=gw, dils=dils, nt=nt, ka=ka, kb=kb),
        grid=(n_tiles + 1,),
        in_specs=[
            pl.BlockSpec((tm, d), cur),
            pl.BlockSpec((1, d), fixed),
            pl.BlockSpec((None, d, n), lambda i: (layer, 0, 0), pipeline_mode=pl.Buffered(1)),
            pl.BlockSpec((ka, sub, ca), lambda i: (0, 0, 0)), pl.BlockSpec((1, ca), fixed),
            pl.BlockSpec((1, ca), fixed), pl.BlockSpec((1, ca), fixed),
            pl.BlockSpec((kb, cb), fixed),
        ],
        out_specs=[
            pl.BlockSpec((tm, d), cur),
            pl.BlockSpec((tm, ca), behind),
            pl.BlockSpec((tm, cb), behind),
        ] + qkv_specs,
        out_shape=[
            jax.ShapeDtypeStruct((t, d), BF16),
            jax.ShapeDtypeStruct((t, ca), BF16),
            jax.ShapeDtypeStruct((t, cb), BF16),
        ] + qkv_shapes,
        scratch_shapes=[
            pltpu.VMEM((gw // V7X_LANES, tm, V7X_LANES), F32),
            pltpu.VMEM((hl, ca), BF16), pltpu.VMEM((tm, ca), BF16),
            pltpu.VMEM((hl, cb), BF16), pltpu.VMEM((tm, cb), BF16),
            pltpu.VMEM((tm, cb), BF16),
            pltpu.VMEM((sub, span, ca), F32),
            pltpu.VMEM((tm + 2 * hl, cb), F32),
        ],
        compiler_params=_params("arbitrary"),
        name="in_proj_conv",
    )(x2, g, w_all, jnp.broadcast_to(wa[:, None, :], (ka, sub, ca)), ba, lg, lb, wb)


ATT_QROWS = 128


def _attn_kernel(q_ref, kc, kp, kn, vc, vp, vn, o_ref, lse_ref, kbuf, vbuf, *, tq, sub_len, dil, half, slopes):
    i = pl.program_id(2)
    kbuf[0:half, :] = kp[0]
    kbuf[half:half + tq, :] = kc[0]
    kbuf[half + tq:half + tq + half, :] = kn[0]
    vbuf[0:half, :] = vp[0]
    vbuf[half:half + tq, :] = vc[0]
    vbuf[half + tq:half + tq + half, :] = vn[0]

    qr = ATT_QROWS
    kw = qr + 2 * half
    lanes = V7X_LANES
    heads_per_pair = lanes // HEAD_DIM
    rel = (lax.broadcasted_iota(jnp.int32, (qr, kw), 1) - half
           - lax.broadcasted_iota(jnp.int32, (qr, kw), 0))
    arel = jnp.abs(rel)
    band = arel <= half
    dist = (dil * arel).astype(F32)
    biases = [jnp.where(band, -(slope * LOG2_E) * dist, NEG) for slope in slopes]
    lane_head = lax.broadcasted_iota(jnp.int32, (1, lanes), 1) // HEAD_DIM

    for o in range(0, tq, qr):
        at_edge = o == 0 or o + qr == tq
        if at_edge:
            kpos = i * tq + (o - half) + lax.broadcasted_iota(jnp.int32, (1, kw), 1)
            pen = jnp.where((kpos >= 0) & (kpos < sub_len), 0.0, NEG)
        for pair in range(q_ref.shape[2] // lanes):
            cols = slice(pair * lanes, (pair + 1) * lanes)
            qp = q_ref[0, o:o + qr, cols]
            kpair = kbuf[o:o + kw, cols]
            vpair = vbuf[o:o + kw, cols]
            out = None
            lse = None
            for hh in range(heads_per_pair):
                sel = lane_head == hh
                qm = jnp.where(sel, qp, jnp.zeros_like(qp))
                s = lax.dot_general(qm, kpair, (((1,), (1,)), ((), ())), preferred_element_type=F32)
                s = s + biases[pair * heads_per_pair + hh]
                if at_edge:
                    s = s + pen
                m = jnp.max(s, axis=-1, keepdims=True)
                p = jnp.exp2(s - m)
                l = jnp.sum(p, axis=-1, keepdims=True)
                pv = _dot(p.astype(BF16), vpair) * (1.0 / l)
                hl = (m + jnp.log2(l)) * LN_2
                out = pv if out is None else jnp.where(sel, pv, out)
                lse = jnp.broadcast_to(hl, pv.shape) if lse is None else jnp.where(sel, hl, lse)
            o_ref[0, o:o + qr, cols] = out.astype(BF16)
            lse_ref[0, o:o + qr, cols] = lse


def _dilated_attention(qkv, g):
    bsz, dil, sub_len, w3 = qkv.shape
    window, gdil = DIL_GROUPS[g]
    assert gdil == dil
    half = window // (2 * dil)
    gw = w3 // 3
    assert sub_len % half == 0
    tq = _tile(sub_len, 2048)
    assert tq % ATT_QROWS == 0 and tq % half == 0 and half % V7X_BF16_SUBLANES == 0
    nh = sub_len // half
    slopes = tuple(float(2.0 ** (-8.0 * (g * HEADS_PER_GROUP + h + 1) / N_HEADS)) for h in range(HEADS_PER_GROUP))

    def spec(rows, which, row_map):
        return pl.BlockSpec((1, None, rows, gw), lambda b, c, i: (b, c, row_map(i), which))

    cur = lambda i: i
    prev = lambda i: jnp.maximum(i * (tq // half) - 1, 0)
    nxt = lambda i: jnp.minimum((i + 1) * (tq // half), nh - 1)
    out_spec = pl.BlockSpec((1, None, tq, gw), lambda b, c, i: (b, c, i, 0))
    return pl.pallas_call(
        functools.partial(_attn_kernel, tq=tq, sub_len=sub_len, dil=dil, half=half, slopes=slopes),
        grid=(bsz, dil, sub_len // tq),
        in_specs=[
            spec(tq, 0, cur),
            spec(tq, 1, cur), spec(half, 1, prev), spec(half, 1, nxt),
            spec(tq, 2, cur), spec(half, 2, prev), spec(half, 2, nxt),
        ],
        out_specs=[out_spec, out_spec],
        out_shape=[jax.ShapeDtypeStruct((bsz, dil, sub_len, gw), BF16),
                   jax.ShapeDtypeStruct((bsz, dil, sub_len, gw), F32)],
        scratch_shapes=[pltpu.VMEM((tq + 2 * half, gw), BF16), pltpu.VMEM((tq + 2 * half, gw), BF16)],
        compiler_params=_params("parallel", "parallel", "parallel"),
        name=f"dilated_attention_g{g}",
    )(qkv, qkv, qkv, qkv, qkv, qkv, qkv)


def _natural_order(ref, nat_ref):
    _, dil, rows, w = ref.shape
    if dil == 1:
        return ref[0, 0].astype(F32)
    lanes = V7X_LANES
    for c in range(dil):
        for s in range(w // lanes):
            nat_ref[s, pl.ds(c, rows, stride=dil), :] = ref[0, c, :, s * lanes:(s + 1) * lanes].astype(F32)
    return jnp.concatenate([nat_ref[s] for s in range(w // lanes)], axis=1)


def _mixer_kernel(x_ref, h_ref, a_ref, b_ref, o0, o1, o2, l0, l1, l2, wg_ref, wbr_ref, bg_ref, wo_ref,
                  out_ref, att_ref, nat_ref):
    j = pl.program_id(1)

    @pl.when(j == 0)
    def _():
        ls = [_natural_order(l, nat_ref) for l in (l0, l1, l2)]
        m = jnp.maximum(jnp.maximum(ls[0], ls[1]), ls[2])
        es = [jnp.exp(l - m) for l in ls]
        inv = 1.0 / (es[0] + es[1] + es[2])
        num = None
        for e, o in zip(es, (o0, o1, o2)):
            term = e * _natural_order(o, nat_ref)
            num = term if num is None else num + term
        att_ref[...] = (num * inv).astype(BF16)

    tn = wo_ref.shape[0]
    ca, cb = a_ref.shape[1], b_ref.shape[1]

    def projected_mix():
        gates = jax.nn.sigmoid(_dot(h_ref[...], wg_ref[...]) + bg_ref[...])
        mix = (gates[:, 0:tn] * _dot(a_ref[...], wbr_ref[0:ca, :])
               + gates[:, tn:2 * tn] * _dot(b_ref[...], wbr_ref[ca:ca + cb, :])
               + gates[:, 2 * tn:3 * tn] * _dot(att_ref[...], wbr_ref[ca + cb:, :]))
        return _dot(mix.astype(BF16), wo_ref[...])

    @pl.when(j == 0)
    def _():
        out_ref[...] = x_ref[...] + projected_mix()

    @pl.when(j > 0)
    def _():
        out_ref[...] += projected_mix()


def _mixer(x2, h, a, b, os_, ls_, layer, w_gate, w_branch, b_gate, wo):
    t, d = x2.shape
    ca, cb = a.shape[1], b.shape[1]
    bsz, _, _, gw = os_[0].shape
    s = t // bsz
    tm = _tile(s, 512)
    nt = s // tm
    nj, tn = w_branch.shape[1], w_branch.shape[3]
    assert nj * tn == d and w_gate.shape[1:] == (nj, d, 3 * tn) and w_branch.shape[2] == ca + cb + gw
    row = lambda i, j: (i, 0)
    chunk = lambda i, j: (layer, j, 0, 0)
    att_specs = [pl.BlockSpec((1, o.shape[1], tm // o.shape[1], gw), lambda i, j: (i // nt, 0, i % nt, 0))
                 for o in os_]
    return pl.pallas_call(
        _mixer_kernel,
        grid=(t // tm, nj),
        in_specs=[
            pl.BlockSpec((tm, d), row), pl.BlockSpec((tm, d), row),
            pl.BlockSpec((tm, ca), row), pl.BlockSpec((tm, cb), row),
            *att_specs, *att_specs,
            pl.BlockSpec((None, None, d, 3 * tn), chunk),
            pl.BlockSpec((None, None, ca + cb + gw, tn), chunk),
            pl.BlockSpec((None, None, 1, 3 * tn), chunk),
            pl.BlockSpec((None, tn, d), lambda i, j: (layer, j, 0)),
        ],
        out_specs=pl.BlockSpec((tm, d), row),
        out_shape=jax.ShapeDtypeStruct((t, d), F32),
        scratch_shapes=[pltpu.VMEM((tm, gw), BF16), pltpu.VMEM((gw // V7X_LANES, tm, V7X_LANES), F32)],
        compiler_params=_params("parallel", "arbitrary"),
        name="gated_mixer",
    )(x2, h, a, b, *os_, *ls_, w_gate, w_branch, b_gate, wo)


def _ffn_kernel(x_ref, g_ref, wgu_ref, wd_ref, *rest, final):
    if final:
        fg_ref, out_ref, h_ref = rest
    else:
        out_ref, h_ref = rest
    j = pl.program_id(1)

    tf = wd_ref.shape[0]

    def down_projected():
        gu = _dot(h_ref[...], wgu_ref[...])
        gate = gu[:, 0:tf]
        f = gate * jax.nn.sigmoid(gate) * gu[:, tf:2 * tf]
        return _dot(f.astype(BF16), wd_ref[...])

    @pl.when(j == 0)
    def _():
        h_ref[...] = _rms_normed(x_ref[...], g_ref[...]).astype(BF16)
        out_ref[...] = x_ref[...] + down_projected()

    @pl.when(j > 0)
    def _():
        out_ref[...] += down_projected()

    if final:
        @pl.when(j == pl.num_programs(1) - 1)
        def _():
            out_ref[...] = _rms_normed(out_ref[...], fg_ref[...])


def _ffn(x2, g, layer, w_gate_up, wd, final_g=None):
    t, d = x2.shape
    nf, tf = w_gate_up.shape[1], w_gate_up.shape[3] // 2
    assert nf * tf == wd.shape[1]
    tm = _tile(t, 1024)
    row = lambda i, j: (i, 0)
    fixed = lambda i, j: (0, 0)
    final = final_g is not None
    in_specs = [
        pl.BlockSpec((tm, d), row), pl.BlockSpec((1, d), fixed),
        pl.BlockSpec((None, None, d, 2 * tf), lambda i, j: (layer, j, 0, 0)),
        pl.BlockSpec((None, tf, d), lambda i, j: (layer, j, 0)),
    ]
    args = [x2, g, w_gate_up, wd]
    if final:
        in_specs.append(pl.BlockSpec((1, d), fixed))
        args.append(final_g)
    return pl.pallas_call(
        functools.partial(_ffn_kernel, final=final),
        grid=(t // tm, nf),
        in_specs=in_specs,
        out_specs=pl.BlockSpec((tm, d), row),
        out_shape=jax.ShapeDtypeStruct((t, d), F32),
        scratch_shapes=[pltpu.VMEM((tm, d), BF16)],
        compiler_params=_params("parallel", "arbitrary"),
        name="swiglu_ffn_final" if final else "swiglu_ffn",
    )(*args)


def kernel(x, norm1_g, w_in, b_gate, conv_a_w, conv_a_b, ln_a_g, ln_a_b, w_a_out, conv_b_w, w_b_out,
           w_c_out, w_o, norm2_g, w_ffn_gate, w_ffn_up, w_ffn_down, final_g):
    bsz, s, d = x.shape
    depth = w_in.shape[0]
    ca, cb = conv_a_w.shape[2], conv_b_w.shape[2]
    n_proj = w_in.shape[2] - 3 * d
    aw = (n_proj - 2 * ca - 3 * cb) // 3
    dff = w_ffn_gate.shape[2]
    assert aw == N_HEADS * HEAD_DIM and w_c_out.shape[1] == ATT_OUT

    tn = _tile(d, MIXER_COLS)
    nj = d // tn
    tf = _tile(dff, FFN_COLS)
    gate_chunks = [[(0, n_proj + k * d + j * tn, tn) for k in range(3)] for j in range(nj)]
    w_proj, w_gate = _repack_bf16([w_in], [("plain", 0, 0, n_proj), ("chunked", gate_chunks)])
    b_gate_c = b_gate.reshape(depth, 3, nj, tn).transpose(0, 2, 1, 3).reshape(depth, nj, 1, 3 * tn)
    w_branch = _stack_rows_bf16([w_a_out, w_b_out, w_c_out], tn)
    wo = _cast_bf16(w_o)
    w_gate_up = _repack_bf16(
        [w_ffn_gate, w_ffn_up], [("chunked", [[(0, j * tf, tf), (1, j * tf, tf)] for j in range(dff // tf)])])[0]
    w_down = _cast_bf16(w_ffn_down)

    row = lambda v: v.reshape(1, -1)
    x2 = x.reshape(bsz * s, d)
    for l in range(depth):
        h, a_act, b_act, *qkvs = _in_proj(
            x2, row(norm1_g[l]), w_proj, l, conv_a_w[l], row(conv_a_b[l]), row(ln_a_g[l]), row(ln_a_b[l]),
            conv_b_w[l], aw, bsz)
        att = [_dilated_attention(qkv.reshape(bsz, dil, s // dil, 3 * ATT_OUT), g)
               for g, (qkv, (_, dil)) in enumerate(zip(qkvs, DIL_GROUPS))]
        x2 = _mixer(x2, h, a_act, b_act, [o for o, _ in att], [ls for _, ls in att],
                    l, w_gate, w_branch, b_gate_c, wo)
        x2 = _ffn(x2, row(norm2_g[l]), l, w_gate_up, w_down, row(final_g) if l == depth - 1 else None)
    return x2.reshape(bsz, s, d)
```
